```python
import math
import jax
import jax.numpy as jnp
from jax import lax
import numpy as np

D_MODEL = 1024
BATCH = 2
SEQ = 8192
DEPTH = 1

SSM_WIDTH = D_MODEL // 2
SSM_GROUP = 16
SSM_GROUPS = SSM_WIDTH // SSM_GROUP
SSM_STATE = 64
DT_MIN = 1e-3
DT_MAX = 1e-1

N_HEADS = 8
HEAD_DIM = 64
ATTN_WIDTH = N_HEADS * HEAD_DIM
IDX_HEADS = 8
IDX_DIM = 64
TOPK_KEYS = 256
Q_BLOCK = 128

N_EXPERTS = 32
TOP_K = 4
D_FF = D_MODEL
SWIGLU_LIMIT = 7.0
SWIGLU_ALPHA = 1.702
MOE_BLOCK = 128

N_BRANCHES = 2
LN_EPS = 1e-5
DEEPNORM_ALPHA = (2 * DEPTH) ** 0.25
DEEPNORM_BETA = (8 * DEPTH) ** -0.25

IN_SIZES = (SSM_WIDTH, ATTN_WIDTH, ATTN_WIDTH, ATTN_WIDTH, IDX_HEADS * IDX_DIM, IDX_DIM, IDX_HEADS, N_BRANCHES * D_MODEL)
IN_WIDTH = SSM_WIDTH + 3 * ATTN_WIDTH + IDX_HEADS * IDX_DIM + IDX_DIM + IDX_HEADS + N_BRANCHES * D_MODEL

kernel_name = "hybrid_s5_dsa_moe_deepnorm"


def layer_norm(x, g, b):
    xf = x.astype(jnp.float32)
    mu = jnp.mean(xf, axis=-1, keepdims=True)
    var = jnp.mean(jnp.square(xf - mu), axis=-1, keepdims=True)
    y = (xf - mu) * lax.rsqrt(var + LN_EPS) * g.astype(jnp.float32) + b.astype(jnp.float32)
    return y.astype(x.dtype)


def alibi_slopes(n_heads):
    return jnp.asarray([2.0 ** (-8.0 * (hh + 1) / n_heads) for hh in range(n_heads)], jnp.float32)


def s5_branch(u, lam_re, lam_im, log_dt, b_re, b_im, c_re, c_im, d_skip, w_glu):
    f32 = jnp.float32
    bsz, seq, _ = u.shape
    uf = u.astype(f32).reshape(bsz, seq, SSM_GROUPS, SSM_GROUP)
    lam = lax.complex(lam_re.astype(f32), lam_im.astype(f32))
    dt = jnp.exp(log_dt.astype(f32))[:, None]
    a_bar = jnp.exp(lam * dt)
    b_c = lax.complex(b_re.astype(f32), b_im.astype(f32))
    b_bar = ((a_bar - 1.0) / lam)[..., None] * b_c
    bu = jnp.einsum('blgp,gnp->blgn', uf.astype(jnp.complex64), b_bar)
    a_seq = jnp.broadcast_to(a_bar, bu.shape)

    def combine(e1, e2):
        a1, s1 = e1
        a2, s2 = e2
        return a1 * a2, a2 * s1 + s2

    _, states = lax.associative_scan(combine, (a_seq, bu), axis=1)
    c_c = lax.complex(c_re.astype(f32), c_im.astype(f32))
    y = jnp.einsum('blgn,gpn->blgp', states, c_c).real + d_skip.astype(f32) * uf
    y = jax.nn.gelu(y.reshape(bsz, seq, SSM_WIDTH)).astype(u.dtype)
    val, gate = jnp.split(y @ w_glu, 2, axis=-1)
    return val * jax.nn.sigmoid(gate)


def dsa_branch(q, k, v, q_idx, k_idx, w_idx):
    f32 = jnp.float32
    bsz, seq = q.shape[:2]
    n_sel = min(TOPK_KEYS, seq // 4)
    n_blk = seq // Q_BLOCK
    slopes = alibi_slopes(N_HEADS)
    key_pos = jnp.arange(seq, dtype=jnp.int32)
    k_idx_f = k_idx.astype(f32)

    def to_blocks(t):
        return t.reshape(bsz, n_blk, Q_BLOCK, *t.shape[2:]).swapaxes(0, 1)

    def block_fn(args):
        blk, qb, qib, wb = args
        q_pos = blk * Q_BLOCK + jnp.arange(Q_BLOCK, dtype=jnp.int32)
        causal = key_pos[None, :] <= q_pos[:, None]
        dots = jnp.einsum('bqhd,bsd->bhqs', qib.astype(f32), k_idx_f) * IDX_DIM ** -0.5
        score = jnp.einsum('bqh,bhqs->bqs', wb.astype(f32) * IDX_HEADS ** -0.5, jax.nn.relu(dots))
        score = jnp.where(causal[None], score, -jnp.inf)
        _, sel = lax.top_k(score, n_sel)
        k_sel = jax.vmap(lambda kk, ii: kk[ii])(k, sel)
        v_sel = jax.vmap(lambda vv, ii: vv[ii])(v, sel)
        dist = (q_pos[None, :, None] - sel).astype(f32)
        logits = jnp.einsum('bqhd,bqkhd->bhqk', qb.astype(f32), k_sel.astype(f32)) * HEAD_DIM ** -0.5
        logits = logits - slopes[None, :, None, None] * dist[:, None]
        logits = jnp.where((dist >= 0)[:, None], logits, -jnp.inf)
        p = jax.nn.softmax(logits, axis=-1)
        return jnp.einsum('bhqk,bqkhd->bqhd', p.astype(v.dtype), v_sel)

    outs = lax.map(block_fn, (jnp.arange(n_blk, dtype=jnp.int32), to_blocks(q), to_blocks(q_idx), to_blocks(w_idx)))
    return outs.swapaxes(0, 1).reshape(bsz, seq, N_HEADS * HEAD_DIM)


def moe_ffn(x, w_router, b_router, w_up, b_up, w_down, b_down):
    bsz, seq, d = x.shape
    xt = x.reshape(-1, d)
    n_tok = xt.shape[0]
    logits = (xt @ w_router + b_router).astype(jnp.float32)
    top_val, top_idx = lax.top_k(logits, TOP_K)
    gates = jax.nn.softmax(top_val, axis=-1)
    n_assign = n_tok * TOP_K
    flat_e = top_idx.reshape(-1)
    flat_tok = jnp.repeat(jnp.arange(n_tok, dtype=jnp.int32), TOP_K)
    flat_g = gates.reshape(-1)
    order = jnp.argsort(flat_e)
    se, stok, sg = flat_e[order], flat_tok[order], flat_g[order]
    counts = jnp.bincount(flat_e, length=N_EXPERTS)
    starts = jnp.cumsum(counts) - counts
    padded = (counts + MOE_BLOCK - 1) // MOE_BLOCK * MOE_BLOCK
    pends = jnp.cumsum(padded)
    pstarts = pends - padded
    dest = pstarts[se] + jnp.arange(n_assign, dtype=jnp.int32) - starts[se]
    n_rows = n_assign + N_EXPERTS * MOE_BLOCK
    n_blocks = n_rows // MOE_BLOCK
    row_tok = jnp.full((n_rows,), n_tok, jnp.int32).at[dest].set(stok)
    row_gate = jnp.zeros((n_rows,), jnp.float32).at[dest].set(sg)
    blk_start = jnp.arange(n_blocks, dtype=jnp.int32) * MOE_BLOCK
    blk_expert = jnp.minimum(jnp.searchsorted(pends, blk_start, side='right'), N_EXPERTS - 1)

    def expert_block(args):
        e, toks, g = args
        xb = jnp.take(xt, toks, axis=0, mode='clip')
        hdn = xb @ w_up[e] + b_up[e]
        h_gate, h_lin = jnp.split(hdn, 2, axis=-1)
        h_gate = jnp.minimum(h_gate, SWIGLU_LIMIT)
        h_lin = jnp.clip(h_lin, -SWIGLU_LIMIT, SWIGLU_LIMIT)
        act = (h_lin + 1.0) * (h_gate * jax.nn.sigmoid(SWIGLU_ALPHA * h_gate))
        y = act @ w_down[e] + b_down[e]
        return y * g[:, None].astype(y.dtype)

    ys = lax.map(expert_block, (blk_expert, row_tok.reshape(n_blocks, MOE_BLOCK), row_gate.reshape(n_blocks, MOE_BLOCK)))
    out = jnp.zeros_like(xt).at[row_tok].add(ys.reshape(n_rows, d), mode='drop')
    return out.reshape(bsz, seq, d)


def setup_inputs(seed: int = 0) -> dict:
    key = jax.random.key(seed)
    ks = jax.random.split(key, 26)
    f32 = jnp.float32

    def nrm(k, shape, scale):
        return scale * jax.random.normal(k, shape, f32)

    nl, d = DEPTH, D_MODEL
    n_idx = jnp.arange(SSM_STATE, dtype=f32)
    g_shape = (nl, SSM_GROUPS, SSM_STATE)
    return {
        "x": nrm(ks[0], (BATCH, SEQ, d), 1.0),
        "ln_in_g": 1.0 + nrm(ks[1], (d,), 0.02),
        "ln_in_b": nrm(ks[2], (d,), 0.02),
        "w_in": nrm(ks[3], (nl, d, IN_WIDTH), d ** -0.5),
        "b_gate": nrm(ks[4], (nl, N_BRANCHES * d), 0.02),
        "lam_re": -0.5 + nrm(ks[5], g_shape, 0.01),
        "lam_im": math.pi * n_idx + nrm(ks[6], g_shape, 0.01),
        "log_dt": jax.random.uniform(ks[7], (nl, SSM_GROUPS), f32, math.log(DT_MIN), math.log(DT_MAX)),
        "b_re": nrm(ks[8], (nl, SSM_GROUPS, SSM_STATE, SSM_GROUP), (2 * SSM_GROUP) ** -0.5),
        "b_im": nrm(ks[9], (nl, SSM_GROUPS, SSM_STATE, SSM_GROUP), (2 * SSM_GROUP) ** -0.5),
        "c_re": nrm(ks[10], (nl, SSM_GROUPS, SSM_GROUP, SSM_STATE), 0.5),
        "c_im": nrm(ks[11], (nl, SSM_GROUPS, SSM_GROUP, SSM_STATE), 0.5),
        "d_skip": nrm(ks[12], (nl, SSM_GROUPS, SSM_GROUP), 1.0),
        "w_glu": nrm(ks[13], (nl, SSM_WIDTH, 2 * d), SSM_WIDTH ** -0.5),
        "w_attn_proj": nrm(ks[14], (nl, ATTN_WIDTH, d), ATTN_WIDTH ** -0.5),
        "w_o": nrm(ks[15], (nl, d, d), DEEPNORM_BETA * d ** -0.5),
        "ln1_g": 1.0 + nrm(ks[16], (nl, d), 0.02),
        "ln1_b": nrm(ks[17], (nl, d), 0.02),
        "w_router": nrm(ks[18], (nl, d, N_EXPERTS), d ** -0.5),
        "b_router": nrm(ks[19], (nl, N_EXPERTS), 0.01),
        "w_up": nrm(ks[20], (nl, N_EXPERTS, d, 2 * D_FF), d ** -0.5),
        "b_up": nrm(ks[21], (nl, N_EXPERTS, 2 * D_FF), 0.02),
        "w_down": nrm(ks[22], (nl, N_EXPERTS, D_FF, d), DEEPNORM_BETA * D_FF ** -0.5),
        "b_down": nrm(ks[23], (nl, N_EXPERTS, d), 0.02),
        "ln2_g": 1.0 + nrm(ks[24], (nl, d), 0.02),
        "ln2_b": nrm(ks[25], (nl, d), 0.02),
    }


def reference(x, ln_in_g, ln_in_b, w_in, b_gate, lam_re, lam_im, log_dt, b_re, b_im, c_re, c_im, d_skip, w_glu, w_attn_proj, w_o, ln1_g, ln1_b, w_router, b_router, w_up, b_up, w_down, b_down, ln2_g, ln2_b):
    bsz, seq, _ = x.shape
    split_points = np.cumsum(np.array(IN_SIZES))[:-1].tolist()
    h = layer_norm(x, ln_in_g, ln_in_b)
    for l in range(DEPTH):
        proj = h @ w_in[l]
        u, q, k, v, q_idx, k_idx, w_idx, gate_pre = jnp.split(proj, split_points, axis=-1)
        y_ssm = s5_branch(u, lam_re[l], lam_im[l], log_dt[l], b_re[l], b_im[l], c_re[l], c_im[l], d_skip[l], w_glu[l])
        y_att = dsa_branch(q.reshape(bsz, seq, N_HEADS, HEAD_DIM),
                           k.reshape(bsz, seq, N_HEADS, HEAD_DIM),
                           v.reshape(bsz, seq, N_HEADS, HEAD_DIM),
                           q_idx.reshape(bsz, seq, IDX_HEADS, IDX_DIM),
                           k_idx, w_idx) @ w_attn_proj[l]
        g_ssm, g_att = jnp.split(jax.nn.sigmoid(gate_pre + b_gate[l]), N_BRANCHES, axis=-1)
        mix = (g_ssm * y_ssm + g_att * y_att) @ w_o[l]
        h = layer_norm(DEEPNORM_ALPHA * h + mix, ln1_g[l], ln1_b[l])
        ffn = moe_ffn(h, w_router[l], b_router[l], w_up[l], b_up[l], w_down[l], b_down[l])
        h = layer_norm(DEEPNORM_ALPHA * h + ffn, ln2_g[l], ln2_b[l])
    return h
```

```python
import functools
import math

import jax
import jax.numpy as jnp
from jax import lax
from jax.experimental import pallas as pl
from jax.experimental.pallas import tpu as pltpu

F32 = jnp.float32
BF16 = jnp.bfloat16
I32 = jnp.int32

D_MODEL = 1024
SSM_WIDTH = 512
SSM_P = 16
SSM_G = 32
SSM_N = 64
SSM_CHUNK = 16
N_HEADS = 8
HEAD_DIM = 64
ATTN_WIDTH = 512
IDX_HEADS = 8
IDX_DIM = 64
TOPK_KEYS = 256
N_EXPERTS = 32
TOP_K = 4
D_FF = 1024
SWIGLU_LIMIT = 7.0
SWIGLU_ALPHA = 1.702
LN_EPS = 1e-5
DEPTH = 1
DEEPNORM_ALPHA = (2 * DEPTH) ** 0.25

LANES = 128
SUBLANES = 8
ATT_TQ = 256
ATT_TK = 256
MOE_ROWS = 256
INT_MIN = -(2 ** 31)
NEG_BIG = -1e30
LOG2E = math.log2(math.e)
VMEM_LIMIT = 56 * 1024 * 1024


def _cparams(sem, vmem=VMEM_LIMIT):
    return pltpu.CompilerParams(dimension_semantics=sem, vmem_limit_bytes=vmem)


def _layer_norm(x, g, b):
    mu = jnp.mean(x, axis=-1, keepdims=True)
    xc = x - mu
    var = jnp.mean(xc * xc, axis=-1, keepdims=True)
    return xc * lax.rsqrt(var + LN_EPS) * g + b


def _inproj_kernel(x_ref, g_ref, b_ref, wu_ref, wq_ref, wk_ref, wv_ref, wqi_ref, wsm_ref, wg_ref,
                   sms_ref, u_ref, q_ref, k_ref, v_ref, qi_ref, sm_ref, gp_ref):
    h = _layer_norm(x_ref[...], g_ref[...], b_ref[...])
    hb = h.astype(BF16)

    def proj(w_ref):
        return jnp.dot(hb, w_ref[...], preferred_element_type=F32)

    u_ref[...] = proj(wu_ref)
    q_ref[...] = (proj(wq_ref) * (HEAD_DIM ** -0.5 * LOG2E)).astype(BF16)
    k_ref[...] = proj(wk_ref).astype(BF16)
    v_ref[...] = proj(wv_ref).astype(BF16)
    qi_ref[...] = proj(wqi_ref).astype(BF16)
    sm_ref[...] = proj(wsm_ref) * sms_ref[...]
    gp_ref[...] = proj(wg_ref).astype(BF16)


def _inproj(xt, ln_g, ln_b, w_in):
    t, d = xt.shape
    tm = 512
    o = 0
    ws = []
    for n in (SSM_WIDTH, ATTN_WIDTH, ATTN_WIDTH, ATTN_WIDTH, IDX_HEADS * IDX_DIM):
        ws.append(w_in[:, o:o + n].astype(BF16))
        o += n
    n_small = IDX_DIM + IDX_HEADS
    w_small = jnp.pad(w_in[:, o:o + n_small], ((0, 0), (0, LANES - n_small))).astype(BF16)
    o += n_small
    w_gate = w_in[:, o:].astype(BF16)
    sm_scale = jnp.concatenate([
        jnp.ones((IDX_DIM,), F32),
        jnp.full((IDX_HEADS,), IDX_DIM ** -0.5 * IDX_HEADS ** -0.5, F32),
        jnp.zeros((LANES - n_small,), F32)]).reshape(1, LANES)
    full = lambda a: pl.BlockSpec(a.shape, lambda i: (0,) * a.ndim)
    row = lambda n: pl.BlockSpec((tm, n), lambda i: (i, 0))
    ins = [xt, ln_g.reshape(1, d), ln_b.reshape(1, d), *ws, w_small, w_gate, sm_scale]
    outs = [(SSM_WIDTH, F32), (ATTN_WIDTH, BF16), (ATTN_WIDTH, BF16), (ATTN_WIDTH, BF16),
            (IDX_HEADS * IDX_DIM, BF16), (LANES, F32), (2 * d, BF16)]
    return pl.pallas_call(
        _inproj_kernel,
        grid=(t // tm,),
        in_specs=[row(d)] + [full(a) for a in ins[1:]],
        out_specs=[row(n) for n, _ in outs],
        out_shape=[jax.ShapeDtypeStruct((t, n), dt) for n, dt in outs],
        compiler_params=_cparams(("parallel",)),
        name="inproj",
    )(*ins)


def _ssm_kernel(u_ref, lrr_ref, lir_ref, lrc_ref, lic_ref, ldt_ref, btr_ref, bti_ref, ctr_ref, cti_ref,
                y_ref, tg_sc, sin_sc, buf_a, buf_b, *, n_batch, n_chunk, pad):
    cc, p, n = SSM_CHUNK, SSM_P, SSM_N
    hi = lax.Precision.HIGHEST
    dt = jnp.exp(ldt_ref[...])
    ar, ai = lrr_ref[...] * dt, lir_ref[...] * dt
    mag = jnp.exp(ar)
    a_re, a_im = mag * jnp.cos(ai), mag * jnp.sin(ai)
    lr, li = lrr_ref[...], lir_ref[...]
    den = lr * lr + li * li
    f_re = ((a_re - 1.0) * lr + a_im * li) / den
    f_im = (a_im * lr - (a_re - 1.0) * li) / den
    bb_re = f_re * btr_ref[...] - f_im * bti_ref[...]
    bb_im = f_re * bti_ref[...] + f_im * btr_ref[...]
    arc, aic = lrc_ref[...] * dt, lic_ref[...] * dt
    tau = (lax.broadcasted_iota(I32, (1, cc * p), 1) // p).astype(F32)
    rmag = jnp.exp(arc * tau)
    pw_re, pw_im = rmag * jnp.cos(aic * tau), rmag * jnp.sin(aic * tau)
    r_re = pw_re * ctr_ref[...] - pw_im * cti_ref[...]
    r_im = pw_re * cti_ref[...] + pw_im * ctr_ref[...]
    m0 = (jnp.dot(bb_re, r_re, precision=hi, preferred_element_type=F32)
          - jnp.dot(bb_im, r_im, precision=hi, preferred_element_type=F32))
    lane = lax.broadcasted_iota(I32, (p, cc * p), 1)
    e_pow = (cc - 1 - lax.broadcasted_iota(I32, (cc, 1), 0)).astype(F32)
    pmag = jnp.exp(ar * e_pow)
    ap_re, ap_im = pmag * jnp.cos(ai * e_pow), pmag * jnp.sin(ai * e_pow)
    for s in range(cc):
        blk = m0 if s == 0 else jnp.where(lane >= s * p, pltpu.roll(m0, s * p, axis=1), 0.0)
        tg_sc[s * p:(s + 1) * p, :] = blk.astype(BF16)
        pr, pi = ap_re[s:s + 1, :], ap_im[s:s + 1, :]
        sin_sc[s * p:(s + 1) * p, :] = jnp.concatenate(
            [bb_re * pr - bb_im * pi, bb_re * pi + bb_im * pr], axis=1).astype(BF16)
    mc = jnp.exp(arc)
    ac_re, ac_im = mc * jnp.cos(aic), mc * jnp.sin(aic)
    r1_re = ac_re * r_re - ac_im * r_im
    r1_im = ac_re * r_im + ac_im * r_re
    cout = jnp.concatenate([r1_re, -r1_im], axis=0).astype(BF16)

    u = u_ref[...]
    y_intra = jnp.dot(u, tg_sc[...], preferred_element_type=F32)
    x_loc = jnp.dot(u, sin_sc[...], preferred_element_type=F32)

    seg = pad + n_chunk

    @pl.when(pl.program_id(0) == 0)
    def _():
        for b in range(n_batch):
            buf_a[b * seg:b * seg + pad, :] = jnp.zeros((pad, 2 * n), F32)
            buf_b[b * seg:b * seg + pad, :] = jnp.zeros((pad, 2 * n), F32)

    for b in range(n_batch):
        buf_a[b * seg + pad:(b + 1) * seg, :] = x_loc[b * n_chunk:(b + 1) * n_chunk, :]
    cm = jnp.exp(ar * cc)
    m_re, m_im = cm * jnp.cos(ai * cc), cm * jnp.sin(ai * cc)
    src, dst = buf_a, buf_b
    shift = 1
    while shift < n_chunk:
        m_a = jnp.concatenate([m_re, m_re], axis=1)
        m_b = jnp.concatenate([-m_im, m_im], axis=1)
        for b in range(n_batch):
            base = b * seg + pad
            cur = src[base:base + n_chunk, :]
            prev = src[base - shift:base - shift + n_chunk, :]
            dst[base:base + n_chunk, :] = cur + prev * m_a + pltpu.roll(prev, n, axis=1) * m_b
        m_re, m_im = m_re * m_re - m_im * m_im, 2.0 * m_re * m_im
        src, dst = dst, src
        shift *= 2
    x_in = jnp.concatenate(
        [src[b * seg + pad - 1:b * seg + pad - 1 + n_chunk, :] for b in range(n_batch)], axis=0)
    y_ref[...] = y_intra + jnp.dot(x_in.astype(BF16), cout, preferred_element_type=F32)


def _ssm(u, n_batch, seq, lam_re, lam_im, log_dt, b_re, b_im, c_re, c_im):
    g, p, n, cc = SSM_G, SSM_P, SSM_N, SSM_CHUNK
    n_chunk = seq // cc
    rows = n_batch * n_chunk
    pad = max(n_chunk // 2, SUBLANES)
    ug = (u.astype(BF16).reshape(n_batch, n_chunk, cc, g, p)
          .transpose(3, 0, 1, 2, 4).reshape(g, rows, cc * p))
    ct_re = jnp.tile(c_re.transpose(0, 2, 1), (1, 1, cc))
    ct_im = jnp.tile(c_im.transpose(0, 2, 1), (1, 1, cc))
    ins = [ug, lam_re.reshape(g, 1, n), lam_im.reshape(g, 1, n), lam_re.reshape(g, n, 1),
           lam_im.reshape(g, n, 1), log_dt.reshape(g, 1, 1), b_re.transpose(0, 2, 1),
           b_im.transpose(0, 2, 1), ct_re, ct_im]
    per_group = lambda a: pl.BlockSpec((None,) + a.shape[1:], lambda i: (i, 0, 0))
    y = pl.pallas_call(
        functools.partial(_ssm_kernel, n_batch=n_batch, n_chunk=n_chunk, pad=pad),
        grid=(g,),
        in_specs=[per_group(a) for a in ins],
        out_specs=pl.BlockSpec((None, rows, cc * p), lambda i: (i, 0, 0)),
        out_shape=jax.ShapeDtypeStruct((g, rows, cc * p), F32),
        scratch_shapes=[pltpu.VMEM((cc * p, cc * p), BF16), pltpu.VMEM((cc * p, 2 * n), BF16),
                        pltpu.VMEM((n_batch * (pad + n_chunk), 2 * n), F32),
                        pltpu.VMEM((n_batch * (pad + n_chunk), 2 * n), F32)],
        compiler_params=_cparams(("arbitrary",)),
        name="ssm",
    )(*ins)
    return (y.reshape(g, n_batch, n_chunk, cc, p).transpose(1, 2, 3, 0, 4)
            .reshape(n_batch * seq, g * p))


def _dsa_kernel(ki_ref, qit_ref, wt_ref, k_ref, qt_ref, vt_ref, o_ref,
                key_sc, qz_sc, acc_sc, m_sc, l_sc, *, n_sel):
    tq, tk = ATT_TQ, ATT_TK
    i = pl.program_id(1)
    nj = i + 1
    row = lax.broadcasted_iota(I32, (tk, tq), 0)
    col = lax.broadcasted_iota(I32, (tk, tq), 1)
    causal_diag = row <= col

    def sum_rows8(x):
        return x.reshape(tk // SUBLANES, SUBLANES, tq).sum(axis=0)

    def score_block(j, carry):
        kib = ki_ref[j]
        s = jnp.zeros((tk, tq), F32)
        for h in range(IDX_HEADS):
            d = jnp.dot(kib, qit_ref[h * IDX_DIM:(h + 1) * IDX_DIM, :], preferred_element_type=F32)
            s = s + jnp.maximum(d, 0.0) * wt_ref[h:h + 1, :]
        s = jnp.where(jnp.logical_or(j < i, causal_diag), s, -jnp.inf)
        bits = lax.bitcast_convert_type(s, I32)
        key_sc[j] = bits ^ ((bits >> 31) & 0x7FFFFFFF)
        return carry

    lax.fori_loop(0, nj, score_block, 0)

    def count_ge(cand):
        def body(j, acc):
            return acc + sum_rows8((key_sc[j] >= cand).astype(I32))
        acc = lax.fori_loop(0, nj, body, jnp.zeros((SUBLANES, tq), I32))
        return acc.sum(axis=0, keepdims=True)

    c0 = count_ge(jnp.zeros((1, tq), I32))
    t0 = jnp.where(c0 >= n_sel, 0, INT_MIN).astype(I32)

    def bit_step(b, t):
        cand = t + lax.shift_left(jnp.int32(1), 30 - b)
        return jnp.where(count_ge(cand) >= n_sel, cand, t)

    thr = lax.fori_loop(0, 31, bit_step, t0)
    n_gt = count_ge(thr + 1)
    need_eq = (n_sel - n_gt).astype(F32)

    for h in range(N_HEADS):
        pair = qt_ref[(h // 2) * LANES:(h // 2 + 1) * LANES, :]
        half = lax.broadcasted_iota(I32, (LANES, tq), 0) // HEAD_DIM
        qz_sc[h] = jnp.where(half == (h % 2), pair, jnp.zeros_like(pair))
    acc_sc[...] = jnp.zeros_like(acc_sc)
    m_sc[...] = jnp.full(m_sc.shape, NEG_BIG, F32)
    l_sc[...] = jnp.zeros_like(l_sc)
    stril = (row > col).astype(BF16)

    def attend_block(j, eq_seen):
        kb = key_sc[j]
        eq = kb == thr
        eq_f = eq.astype(BF16)
        before = jnp.dot(stril, eq_f, preferred_element_type=F32) + eq_seen
        sel = jnp.logical_or(kb > thr, jnp.logical_and(eq, before < need_eq))
        valid = jnp.logical_and(sel, jnp.logical_or(j < i, causal_diag))
        dist = ((i - j) * tq + col - row).astype(F32)
        for h in range(N_HEADS):
            kt = k_ref[j, :, (h // 2) * LANES:(h // 2 + 1) * LANES]
            lg = jnp.dot(kt, qz_sc[h], preferred_element_type=F32)
            slope = 2.0 ** (-8.0 * (h + 1) / N_HEADS) * LOG2E
            lg = jnp.where(valid, lg - slope * dist, -jnp.inf)
            m_old = m_sc[h:h + 1, :]
            m_new = jnp.maximum(m_old, jnp.max(lg, axis=0, keepdims=True))
            alpha = jnp.exp2(m_old - m_new)
            pr = jnp.exp2(lg - m_new)
            l_sc[h:h + 1, :] = alpha * l_sc[h:h + 1, :] + jnp.sum(pr, axis=0, keepdims=True)
            m_sc[h:h + 1, :] = m_new
            pv = jnp.dot(vt_ref[j, h * HEAD_DIM:(h + 1) * HEAD_DIM, :], pr.astype(BF16),
                         preferred_element_type=F32)
            sl = slice(h * HEAD_DIM, (h + 1) * HEAD_DIM)
            acc_sc[sl, :] = alpha * acc_sc[sl, :] + pv
        return eq_seen + jnp.sum(eq_f.astype(F32), axis=0, keepdims=True)

    lax.fori_loop(0, nj, attend_block, jnp.zeros((1, tq), F32))
    for h in range(N_HEADS):
        sl = slice(h * HEAD_DIM, (h + 1) * HEAD_DIM)
        o_ref[sl, :] = (acc_sc[sl, :] / l_sc[h:h + 1, :]).astype(o_ref.dtype)


def _dsa(q, k, v, qi, ki, wi, n_batch, seq):
    tq, tk = ATT_TQ, ATT_TK
    nkb = seq // tk
    n_sel = min(TOPK_KEYS, seq // 4)
    to_t = lambda a: a.reshape(n_batch, seq, a.shape[-1]).transpose(0, 2, 1)
    qt, qit, wt = to_t(q), to_t(qi), to_t(wi)
    kb = k.reshape(n_batch, nkb, tk, ATTN_WIDTH)
    kib = ki.reshape(n_batch, nkb, tk, IDX_DIM)
    vt = v.reshape(n_batch, nkb, tk, ATTN_WIDTH).transpose(0, 1, 3, 2)
    per_batch = lambda shape: pl.BlockSpec((None,) + shape, lambda b, i: (b, 0, 0, 0),
                                           pipeline_mode=pl.Buffered(1))
    per_q = lambda r: pl.BlockSpec((None, r, tq), lambda b, i: (b, 0, i))
    out = pl.pallas_call(
        functools.partial(_dsa_kernel, n_sel=n_sel),
        grid=(n_batch, seq // tq),
        in_specs=[per_batch((nkb, tk, IDX_DIM)), per_q(IDX_HEADS * IDX_DIM), per_q(IDX_HEADS),
                  per_batch((nkb, tk, ATTN_WIDTH)), per_q(ATTN_WIDTH), per_batch((nkb, ATTN_WIDTH, tk))],
        out_specs=per_q(ATTN_WIDTH),
        out_shape=jax.ShapeDtypeStruct((n_batch, ATTN_WIDTH, seq), BF16),
        scratch_shapes=[pltpu.VMEM((nkb, tk, tq), I32), pltpu.VMEM((N_HEADS, LANES, tq), BF16),
                        pltpu.VMEM((ATTN_WIDTH, tq), F32), pltpu.VMEM((N_HEADS, tq), F32),
                        pltpu.VMEM((N_HEADS, tq), F32)],
        compiler_params=_cparams(("parallel", "arbitrary")),
        name="dsa",
    )(kib, qit, wt, kb, qt, vt)
    return out.transpose(0, 2, 1).reshape(n_batch * seq, ATTN_WIDTH)


def _merge_kernel(x_ref, g0_ref, b0_ref, ys_ref, u_ref, dsk_ref, att_ref, gp_ref, bg_ref,
                  wglu_ref, wap_ref, wo_ref, g1_ref, b1_ref, wr_ref, br_ref,
                  h1_ref, idx_ref, gate_ref):
    d = D_MODEL
    h = _layer_norm(x_ref[...], g0_ref[...], b0_ref[...])
    y = jax.nn.gelu(ys_ref[...] + dsk_ref[...] * u_ref[...])
    glu = jnp.dot(y.astype(BF16), wglu_ref[...], preferred_element_type=F32)
    y_ssm = glu[:, :d] * jax.nn.sigmoid(glu[:, d:])
    y_att = jnp.dot(att_ref[...], wap_ref[...], preferred_element_type=F32)
    gs = jax.nn.sigmoid(gp_ref[...].astype(F32) + bg_ref[...])
    mixed = gs[:, :d] * y_ssm + gs[:, d:] * y_att
    mix = jnp.dot(mixed.astype(BF16), wo_ref[...], preferred_element_type=F32)
    h1 = _layer_norm(DEEPNORM_ALPHA * h + mix, g1_ref[...], b1_ref[...])
    h1_ref[...] = h1
    logits = jnp.dot(h1, wr_ref[...], precision=lax.Precision.HIGHEST,
                     preferred_element_type=F32) + br_ref[...]
    lane = lax.broadcasted_iota(I32, logits.shape, 1)
    logits = jnp.where(lane < N_EXPERTS, logits, -jnp.inf)
    idx_out = jnp.zeros(logits.shape, I32)
    val_out = jnp.zeros(logits.shape, F32)
    vals = []
    for kk in range(TOP_K):
        vmax = jnp.max(logits, axis=-1, keepdims=True)
        imax = jnp.min(jnp.where(logits == vmax, lane, LANES), axis=-1, keepdims=True)
        idx_out = jnp.where(lane == kk, imax, idx_out)
        vals.append(vmax)
        logits = jnp.where(lane == imax, -jnp.inf, logits)
    es = [jnp.exp(vk - vals[0]) for vk in vals]
    den = es[0] + es[1] + es[2] + es[3]
    for kk in range(TOP_K):
        val_out = jnp.where(lane == kk, es[kk] / den, val_out)
    idx_ref[...] = idx_out
    gate_ref[...] = val_out


def _merge(xt, ln0_g, ln0_b, ys, u, d_skip, att, gp, b_gate, w_glu, w_ap, w_o, ln1_g, ln1_b,
           w_router, b_router):
    t, d = xt.shape
    tm = 256
    wr = jnp.pad(w_router, ((0, 0), (0, LANES - N_EXPERTS)))
    br = jnp.pad(b_router, (0, LANES - N_EXPERTS)).reshape(1, LANES)
    ins = [xt, ln0_g.reshape(1, d), ln0_b.reshape(1, d), ys, u, d_skip.reshape(1, SSM_WIDTH), att, gp,
           b_gate.reshape(1, 2 * d), w_glu.astype(BF16), w_ap.astype(BF16), w_o.astype(BF16),
           ln1_g.reshape(1, d), ln1_b.reshape(1, d), wr, br]
    tiled = {0, 3, 4, 6, 7}
    in_specs = []
    for n, a in enumerate(ins):
        if n in tiled:
            in_specs.append(pl.BlockSpec((tm, a.shape[1]), lambda i: (i, 0)))
        else:
            in_specs.append(pl.BlockSpec(a.shape, lambda i: (0, 0)))
    return pl.pallas_call(
        _merge_kernel,
        grid=(t // tm,),
        in_specs=in_specs,
        out_specs=[pl.BlockSpec((tm, d), lambda i: (i, 0)), pl.BlockSpec((tm, LANES), lambda i: (i, 0)),
                   pl.BlockSpec((tm, LANES), lambda i: (i, 0))],
        out_shape=[jax.ShapeDtypeStruct((t, d), F32), jax.ShapeDtypeStruct((t, LANES), I32),
                   jax.ShapeDtypeStruct((t, LANES), F32)],
        compiler_params=_cparams(("parallel",)),
        name="merge",
    )(*ins)


def _rank_kernel(idx_ref, rank_ref, cnt_ref, run_sc):
    tm = idx_ref.shape[0]

    @pl.when(pl.program_id(0) == 0)
    def _():
        run_sc[...] = jnp.zeros_like(run_sc)

    idx = idx_ref[...]
    lane = lax.broadcasted_iota(I32, (tm, LANES), 1)
    hots = [lane == idx[:, kk:kk + 1] for kk in range(TOP_K)]
    multi = sum(hh.astype(F32) for hh in hots)
    r_i = lax.broadcasted_iota(I32, (tm, tm), 0)
    c_i = lax.broadcasted_iota(I32, (tm, tm), 1)
    before = jnp.dot((c_i < r_i).astype(BF16), multi.astype(BF16), preferred_element_type=F32)
    base = before + run_sc[...]
    out = jnp.zeros((tm, LANES), I32)
    for kk in range(TOP_K):
        rk = jnp.sum(jnp.where(hots[kk], base, 0.0), axis=-1, keepdims=True)
        out = jnp.where(lane == kk, rk.astype(I32), out)
    rank_ref[...] = out
    run_sc[...] = run_sc[...] + jnp.sum(multi, axis=0, keepdims=True)
    cnt_ref[...] = run_sc[...]


def _rank(idx):
    t = idx.shape[0]
    tm = 512
    return pl.pallas_call(
        _rank_kernel,
        grid=(t // tm,),
        in_specs=[pl.BlockSpec((tm, LANES), lambda i: (i, 0))],
        out_specs=[pl.BlockSpec((tm, LANES), lambda i: (i, 0)), pl.BlockSpec((1, LANES), lambda i: (0, 0))],
        out_shape=[jax.ShapeDtypeStruct((t, LANES), I32), jax.ShapeDtypeStruct((1, LANES), F32)],
        scratch_shapes=[pltpu.VMEM((1, LANES), F32)],
        compiler_params=_cparams(("arbitrary",)),
        name="moe_rank",
    )(idx)


def _row_copy(src_ref, src_row, dst_ref, dst_row, sem):
    return pltpu.make_async_copy(src_ref.at[pl.ds(src_row, 1), :], dst_ref.at[pl.ds(dst_row, 1), :], sem)


def _dispatch_kernel(dest_ref, h_ref, xs_in_ref, xs_ref, sem):
    del xs_in_ref
    tm = h_ref.shape[0]
    base = pl.program_id(0) * tm * TOP_K

    def start(r, c):
        for kk in range(TOP_K):
            _row_copy(h_ref, r, xs_ref, dest_ref[base + r * TOP_K + kk], sem).start()
        return c

    lax.fori_loop(0, tm, start, 0)

    def wait(r, c):
        for kk in range(TOP_K):
            _row_copy(h_ref, 0, xs_ref, 0, sem).wait()
        return c

    lax.fori_loop(0, tm, wait, 0)


def _dispatch(dest_flat, h1, n_rows):
    t, d = h1.shape
    tm = 128
    return pl.pallas_call(
        _dispatch_kernel,
        grid_spec=pltpu.PrefetchScalarGridSpec(
            num_scalar_prefetch=1,
            grid=(t // tm,),
            in_specs=[pl.BlockSpec((tm, d), lambda i, dest: (i, 0)), pl.BlockSpec(memory_space=pl.ANY)],
            out_specs=pl.BlockSpec(memory_space=pl.ANY),
            scratch_shapes=[pltpu.SemaphoreType.DMA(())]),
        out_shape=jax.ShapeDtypeStruct((n_rows, d), F32),
        input_output_aliases={2: 0},
        compiler_params=_cparams(("arbitrary",)),
        name="moe_dispatch",
    )(dest_flat, h1, jnp.zeros((n_rows, d), F32))


def _ffn_kernel(be_ref, nused_ref, xs_ref, wup_ref, bup_ref, wdn_ref, bdn_ref, y_ref):
    b = pl.program_id(0)

    @pl.when(b < nused_ref[0])
    def _():
        x = xs_ref[...].astype(BF16)
        hdn = jnp.dot(x, wup_ref[...], preferred_element_type=F32) + bup_ref[...]
        h_gate = jnp.minimum(hdn[:, :D_FF], SWIGLU_LIMIT)
        h_lin = jnp.clip(hdn[:, D_FF:], -SWIGLU_LIMIT, SWIGLU_LIMIT)
        act = (h_lin + 1.0) * (h_gate * jax.nn.sigmoid(SWIGLU_ALPHA * h_gate))
        y_ref[...] = jnp.dot(act.astype(BF16), wdn_ref[...], preferred_element_type=F32) + bdn_ref[...]

    @pl.when(b >= nused_ref[0])
    def _():
        y_ref[...] = jnp.zeros_like(y_ref)


def _ffn(blk_expert, n_used, xs, w_up, b_up, w_down, b_down):
    n_rows, d = xs.shape
    r = MOE_ROWS
    e = N_EXPERTS
    return pl.pallas_call(
        _ffn_kernel,
        grid_spec=pltpu.PrefetchScalarGridSpec(
            num_scalar_prefetch=2,
            grid=(n_rows // r,),
            in_specs=[pl.BlockSpec((r, d), lambda b, be, nu: (b, 0)),
                      pl.BlockSpec((None, d, 2 * D_FF), lambda b, be, nu: (be[b], 0, 0)),
                      pl.BlockSpec((None, 1, 2 * D_FF), lambda b, be, nu: (be[b], 0, 0)),
                      pl.BlockSpec((None, D_FF, d), lambda b, be, nu: (be[b], 0, 0)),
                      pl.BlockSpec((None, 1, d), lambda b, be, nu: (be[b], 0, 0))],
            out_specs=pl.BlockSpec((r, d), lambda b, be, nu: (b, 0))),
        out_shape=jax.ShapeDtypeStruct((n_rows, d), F32),
        compiler_params=_cparams(("arbitrary",)),
        name="moe_ffn",
    )(blk_expert, n_used, xs, w_up.astype(BF16), b_up.reshape(e, 1, 2 * D_FF),
      w_down.astype(BF16), b_down.reshape(e, 1, d))


def _combine_kernel(dest_ref, h1_ref, gate_ref, g2_ref, b2_ref, y_ref, o_ref, buf, sem):
    tm = h1_ref.shape[0]
    base = pl.program_id(0) * tm * TOP_K

    def start(r, c):
        for kk in range(TOP_K):
            _row_copy(y_ref, dest_ref[base + r * TOP_K + kk], buf.at[kk], r, sem).start()
        return c

    lax.fori_loop(0, tm, start, 0)

    def wait(r, c):
        for kk in range(TOP_K):
            _row_copy(y_ref, 0, buf.at[kk], 0, sem).wait()
        return c

    lax.fori_loop(0, tm, wait, 0)
    gates = gate_ref[...]
    ffn = jnp.zeros(h1_ref.shape, F32)
    for kk in range(TOP_K):
        ffn = ffn + buf[kk] * gates[:, kk:kk + 1]
    o_ref[...] = _layer_norm(DEEPNORM_ALPHA * h1_ref[...] + ffn, g2_ref[...], b2_ref[...])


def _combine(dest_flat, h1, gates, ln2_g, ln2_b, y):
    t, d = h1.shape
    tm = 128
    return pl.pallas_call(
        _combine_kernel,
        grid_spec=pltpu.PrefetchScalarGridSpec(
            num_scalar_prefetch=1,
            grid=(t // tm,),
            in_specs=[pl.BlockSpec((tm, d), lambda i, dest: (i, 0)),
                      pl.BlockSpec((tm, LANES), lambda i, dest: (i, 0)),
                      pl.BlockSpec((1, d), lambda i, dest: (0, 0)),
                      pl.BlockSpec((1, d), lambda i, dest: (0, 0)),
                      pl.BlockSpec(memory_space=pl.ANY)],
            out_specs=pl.BlockSpec((tm, d), lambda i, dest: (i, 0)),
            scratch_shapes=[pltpu.VMEM((TOP_K, tm, d), F32), pltpu.SemaphoreType.DMA(())]),
        out_shape=jax.ShapeDtypeStruct((t, d), F32),
        compiler_params=_cparams(("arbitrary",)),
        name="moe_combine",
    )(dest_flat, h1, gates, ln2_g.reshape(1, d), ln2_b.reshape(1, d), y)


def kernel(x, ln_in_g, ln_in_b, w_in, b_gate, lam_re, lam_im, log_dt, b_re, b_im, c_re, c_im, d_skip,
           w_glu, w_attn_proj, w_o, ln1_g, ln1_b, w_router, b_router, w_up, b_up, w_down, b_down,
           ln2_g, ln2_b):
    n_batch, seq, d = x.shape
    assert d == D_MODEL and seq % ATT_TQ == 0 and w_in.shape[0] == DEPTH
    t = n_batch * seq
    xt = x.reshape(t, d)

    u, q, k, v, qi, sm, gp = _inproj(xt, ln_in_g, ln_in_b, w_in[0])
    ki = sm[:, :IDX_DIM].astype(BF16)
    wi = sm[:, IDX_DIM:IDX_DIM + IDX_HEADS]

    ys = _ssm(u, n_batch, seq, lam_re[0], lam_im[0], log_dt[0], b_re[0], b_im[0], c_re[0], c_im[0])
    att = _dsa(q, k, v, qi, ki, wi, n_batch, seq)
    h1, idx, gates = _merge(xt, ln_in_g, ln_in_b, ys, u, d_skip[0], att, gp, b_gate[0], w_glu[0],
                            w_attn_proj[0], w_o[0], ln1_g[0], ln1_b[0], w_router[0], b_router[0])

    rank, counts = _rank(idx)
    counts = counts[0, :N_EXPERTS].astype(I32)
    padded = (counts + MOE_ROWS - 1) // MOE_ROWS * MOE_ROWS
    pends = jnp.cumsum(padded)
    pstarts = pends - padded
    top_idx = idx[:, :TOP_K]
    dest = (pstarts[top_idx] + rank[:, :TOP_K]).reshape(-1)
    n_rows = t * TOP_K + N_EXPERTS * MOE_ROWS
    blk_start = jnp.arange(n_rows // MOE_ROWS, dtype=I32) * MOE_ROWS
    blk_expert = jnp.minimum(jnp.searchsorted(pends, blk_start, side="right"), N_EXPERTS - 1).astype(I32)
    n_used = (pends[-1:] // MOE_ROWS).astype(I32)

    xs = _dispatch(dest, h1, n_rows)
    y = _ffn(blk_expert, n_used, xs, w_up[0], b_up[0], w_down[0], b_down[0])
    out = _combine(dest, h1, gates, ln2_g[0], ln2_b[0], y)
    return out.reshape(n_batch, seq, d)
```

```python
import functools
import math

import jax
import jax.numpy as jnp
import numpy as np
from jax import lax
from jax.experimental import pallas as pl
from jax.experimental.pallas import tpu as pltpu

F32 = jnp.float32
BF16 = jnp.bfloat16
I32 = jnp.int32
I16 = jnp.int16

D_MODEL = 1024
SSM_WIDTH = 512
SSM_P = 16
SSM_G = 32
SSM_N = 64
SSM_CHUNK = 16
N_HEADS = 8
HEAD_DIM = 64
ATTN_WIDTH = 512
IDX_HEADS = 8
IDX_DIM = 64
TOPK_KEYS = 256
N_EXPERTS = 32
TOP_K = 4
D_FF = 1024
SWIGLU_LIMIT = 7.0
SWIGLU_ALPHA = 1.702
LN_EPS = 1e-5
DEPTH = 1
DEEPNORM_ALPHA = (2 * DEPTH) ** 0.25

LANES = 128
SUBLANES = 8
ATT_TQ = 256
ATT_TK = 256
ATT_VROWS = HEAD_DIM + 16
ALIBI_SPLIT = 4
MOE_ROWS = 256
BISECT_STEPS = 32
NEG_BIG = -1e30
Q_SCALE = float(np.float32(HEAD_DIM ** -0.5 * math.log2(math.e)))
VMEM_LIMIT = 56 * 1024 * 1024


def _cparams(sem, vmem=VMEM_LIMIT):
    return pltpu.CompilerParams(dimension_semantics=sem, vmem_limit_bytes=vmem)


def _layer_norm(x, g, b):
    mu = jnp.mean(x, axis=-1, keepdims=True)
    xc = x - mu
    var = jnp.mean(xc * xc, axis=-1, keepdims=True)
    return xc * lax.rsqrt(var + LN_EPS) * g + b


def _inproj_kernel(x_ref, g_ref, b_ref, wu_ref, wq_ref, wk_ref, wv_ref, wqi_ref, wsm_ref, wg_ref,
                   sms_ref, u_ref, q_ref, k_ref, v_ref, qi_ref, sm_ref, gp_ref, *, seq):
    h = _layer_norm(x_ref[...], g_ref[...], b_ref[...])
    hb = h.astype(BF16)
    tm = x_ref.shape[0]

    def proj(w_ref):
        return jnp.dot(hb, w_ref[...], preferred_element_type=F32)

    u_ref[...] = proj(wu_ref)
    q_ref[...] = (proj(wq_ref) * Q_SCALE).astype(BF16)
    shape = (tm, N_HEADS * LANES)
    pos = (pl.program_id(0) * tm + lax.broadcasted_iota(I32, shape, 0)) % seq
    sub = lax.broadcasted_iota(I32, shape, 1) % LANES - HEAD_DIM
    feat = jnp.where(jnp.logical_and(sub >= 0, sub < ALIBI_SPLIT), pos % ATT_TK,
                     jnp.where(jnp.logical_and(sub >= ALIBI_SPLIT, sub < 2 * ALIBI_SPLIT), pos // ATT_TK, 0))
    k_ref[...] = (proj(wk_ref) + feat.astype(F32)).astype(BF16)
    v_ref[...] = proj(wv_ref).astype(BF16)
    qi_ref[...] = proj(wqi_ref).astype(BF16)
    sm_ref[...] = proj(wsm_ref) * sms_ref[...]
    gp_ref[...] = proj(wg_ref).astype(BF16)


def _inproj(xt, ln_g, ln_b, w_in, seq):
    t, d = xt.shape
    tm = 512
    o = 0
    ws = []
    for n in (SSM_WIDTH, ATTN_WIDTH, ATTN_WIDTH, ATTN_WIDTH, IDX_HEADS * IDX_DIM):
        ws.append(w_in[:, o:o + n].astype(BF16))
        o += n
    ws[2] = jnp.pad(ws[2].reshape(d, N_HEADS, HEAD_DIM),
                    ((0, 0), (0, 0), (0, LANES - HEAD_DIM))).reshape(d, N_HEADS * LANES)
    n_small = IDX_DIM + IDX_HEADS
    w_small = jnp.pad(w_in[:, o:o + n_small], ((0, 0), (0, LANES - n_small))).astype(BF16)
    o += n_small
    w_gate = w_in[:, o:].astype(BF16)
    sm_scale = jnp.concatenate([
        jnp.ones((IDX_DIM,), F32),
        jnp.full((IDX_HEADS,), IDX_DIM ** -0.5 * IDX_HEADS ** -0.5, F32),
        jnp.zeros((LANES - n_small,), F32)]).reshape(1, LANES)
    full = lambda a: pl.BlockSpec(a.shape, lambda i: (0,) * a.ndim)
    row = lambda n: pl.BlockSpec((tm, n), lambda i: (i, 0))
    ins = [xt, ln_g.reshape(1, d), ln_b.reshape(1, d), *ws, w_small, w_gate, sm_scale]
    outs = [(SSM_WIDTH, F32), (ATTN_WIDTH, BF16), (N_HEADS * LANES, BF16), (ATTN_WIDTH, BF16),
            (IDX_HEADS * IDX_DIM, BF16), (LANES, F32), (2 * d, BF16)]
    return pl.pallas_call(
        functools.partial(_inproj_kernel, seq=seq),
        grid=(t // tm,),
        in_specs=[row(d)] + [full(a) for a in ins[1:]],
        out_specs=[row(n) for n, _ in outs],
        out_shape=[jax.ShapeDtypeStruct((t, n), dt) for n, dt in outs],
        compiler_params=_cparams(("parallel",)),
        name="inproj",
    )(*ins)


def _ssm_kernel(u_ref, lrr_ref, lir_ref, lrc_ref, lic_ref, ldt_ref, btr_ref, bti_ref, ctr_ref, cti_ref,
                y_ref, tg_sc, sin_sc, buf_a, buf_b, *, n_batch, n_chunk, pad):
    cc, p, n = SSM_CHUNK, SSM_P, SSM_N
    hi = lax.Precision.HIGHEST
    dt = jnp.exp(ldt_ref[...])
    ar, ai = lrr_ref[...] * dt, lir_ref[...] * dt
    mag = jnp.exp(ar)
    a_re, a_im = mag * jnp.cos(ai), mag * jnp.sin(ai)
    lr, li = lrr_ref[...], lir_ref[...]
    den = lr * lr + li * li
    f_re = ((a_re - 1.0) * lr + a_im * li) / den
    f_im = (a_im * lr - (a_re - 1.0) * li) / den
    bb_re = f_re * btr_ref[...] - f_im * bti_ref[...]
    bb_im = f_re * bti_ref[...] + f_im * btr_ref[...]
    arc, aic = lrc_ref[...] * dt, lic_ref[...] * dt
    tau = (lax.broadcasted_iota(I32, (1, cc * p), 1) // p).astype(F32)
    rmag = jnp.exp(arc * tau)
    pw_re, pw_im = rmag * jnp.cos(aic * tau), rmag * jnp.sin(aic * tau)
    r_re = pw_re * ctr_ref[...] - pw_im * cti_ref[...]
    r_im = pw_re * cti_ref[...] + pw_im * ctr_ref[...]
    m0 = (jnp.dot(bb_re, r_re, precision=hi, preferred_element_type=F32)
          - jnp.dot(bb_im, r_im, precision=hi, preferred_element_type=F32))
    lane = lax.broadcasted_iota(I32, (p, cc * p), 1)
    e_pow = (cc - 1 - lax.broadcasted_iota(I32, (cc, 1), 0)).astype(F32)
    pmag = jnp.exp(ar * e_pow)
    ap_re, ap_im = pmag * jnp.cos(ai * e_pow), pmag * jnp.sin(ai * e_pow)
    for s in range(cc):
        blk = m0 if s == 0 else jnp.where(lane >= s * p, pltpu.roll(m0, s * p, axis=1), 0.0)
        tg_sc[s * p:(s + 1) * p, :] = blk.astype(BF16)
        pr, pi = ap_re[s:s + 1, :], ap_im[s:s + 1, :]
        sin_sc[s * p:(s + 1) * p, :] = jnp.concatenate(
            [bb_re * pr - bb_im * pi, bb_re * pi + bb_im * pr], axis=1).astype(BF16)
    mc = jnp.exp(arc)
    ac_re, ac_im = mc * jnp.cos(aic), mc * jnp.sin(aic)
    r1_re = ac_re * r_re - ac_im * r_im
    r1_im = ac_re * r_im + ac_im * r_re
    cout = jnp.concatenate([r1_re, -r1_im], axis=0).astype(BF16)

    u = u_ref[...]
    y_intra = jnp.dot(u, tg_sc[...], preferred_element_type=F32)
    x_loc = jnp.dot(u, sin_sc[...], preferred_element_type=F32)

    seg = pad + n_chunk

    @pl.when(pl.program_id(0) == 0)
    def _():
        for b in range(n_batch):
            buf_a[b * seg:b * seg + pad, :] = jnp.zeros((pad, 2 * n), F32)
            buf_b[b * seg:b * seg + pad, :] = jnp.zeros((pad, 2 * n), F32)

    for b in range(n_batch):
        buf_a[b * seg + pad:(b + 1) * seg, :] = x_loc[b * n_chunk:(b + 1) * n_chunk, :]
    cm = jnp.exp(ar * cc)
    m_re, m_im = cm * jnp.cos(ai * cc), cm * jnp.sin(ai * cc)
    src, dst = buf_a, buf_b
    shift = 1
    while shift < n_chunk:
        m_a = jnp.concatenate([m_re, m_re], axis=1)
        m_b = jnp.concatenate([-m_im, m_im], axis=1)
        for b in range(n_batch):
            base = b * seg + pad
            cur = src[base:base + n_chunk, :]
            prev = src[base - shift:base - shift + n_chunk, :]
            dst[base:base + n_chunk, :] = cur + prev * m_a + pltpu.roll(prev, n, axis=1) * m_b
        m_re, m_im = m_re * m_re - m_im * m_im, 2.0 * m_re * m_im
        src, dst = dst, src
        shift *= 2
    x_in = jnp.concatenate(
        [src[b * seg + pad - 1:b * seg + pad - 1 + n_chunk, :] for b in range(n_batch)], axis=0)
    y_ref[...] = y_intra + jnp.dot(x_in.astype(BF16), cout, preferred_element_type=F32)


def _ssm(u, n_batch, seq, lam_re, lam_im, log_dt, b_re, b_im, c_re, c_im):
    g, p, n, cc = SSM_G, SSM_P, SSM_N, SSM_CHUNK
    n_chunk = seq // cc
    rows = n_batch * n_chunk
    pad = max(n_chunk // 2, SUBLANES)
    ug = (u.astype(BF16).reshape(n_batch, n_chunk, cc, g, p)
          .transpose(3, 0, 1, 2, 4).reshape(g, rows, cc * p))
    ct_re = jnp.tile(c_re.transpose(0, 2, 1), (1, 1, cc))
    ct_im = jnp.tile(c_im.transpose(0, 2, 1), (1, 1, cc))
    ins = [ug, lam_re.reshape(g, 1, n), lam_im.reshape(g, 1, n), lam_re.reshape(g, n, 1),
           lam_im.reshape(g, n, 1), log_dt.reshape(g, 1, 1), b_re.transpose(0, 2, 1),
           b_im.transpose(0, 2, 1), ct_re, ct_im]
    per_group = lambda a: pl.BlockSpec((None,) + a.shape[1:], lambda i: (i, 0, 0))
    y = pl.pallas_call(
        functools.partial(_ssm_kernel, n_batch=n_batch, n_chunk=n_chunk, pad=pad),
        grid=(g,),
        in_specs=[per_group(a) for a in ins],
        out_specs=pl.BlockSpec((None, rows, cc * p), lambda i: (i, 0, 0)),
        out_shape=jax.ShapeDtypeStruct((g, rows, cc * p), F32),
        scratch_shapes=[pltpu.VMEM((cc * p, cc * p), BF16), pltpu.VMEM((cc * p, 2 * n), BF16),
                        pltpu.VMEM((n_batch * (pad + n_chunk), 2 * n), F32),
                        pltpu.VMEM((n_batch * (pad + n_chunk), 2 * n), F32)],
        compiler_params=_cparams(("arbitrary",)),
        name="ssm",
    )(*ins)
    return (y.reshape(g, n_batch, n_chunk, cc, p).transpose(1, 2, 3, 0, 4)
            .reshape(n_batch * seq, g * p))


def _dsa_kernel(ki_ref, qit_ref, wt_ref, k_ref, qt_ref, vt_ref, slope_ref, o_ref,
                score_sc, qz_sc, acc_sc, m_sc, bias_sc, lg_sc, p_sc, *, n_sel):
    tq, tk = ATT_TQ, ATT_TK
    i = pl.program_id(1)
    nj = i + 1
    row = lax.broadcasted_iota(I32, (tk, tq), 0)
    col = lax.broadcasted_iota(I32, (tk, tq), 1)
    causal_diag = row <= col

    def fold8(x, op):
        return op(x.reshape(tk // SUBLANES, SUBLANES, tq), axis=0)

    def score_block(j, carry):
        lo8, hi8 = carry
        kib = ki_ref[j]
        s = jnp.zeros((tk, tq), F32)
        for h in range(IDX_HEADS):
            d = jnp.dot(kib, qit_ref[h * IDX_DIM:(h + 1) * IDX_DIM, :], preferred_element_type=F32)
            s = s + jnp.maximum(d, 0.0) * wt_ref[h:h + 1, :]
        causal = jnp.logical_or(j < i, causal_diag)
        score_sc[j] = jnp.where(causal, s, -jnp.inf)
        return (jnp.minimum(lo8, fold8(jnp.where(causal, s, jnp.inf), jnp.min)),
                jnp.maximum(hi8, fold8(jnp.where(causal, s, -jnp.inf), jnp.max)))

    lo8, hi8 = lax.fori_loop(0, nj, score_block, (jnp.full((SUBLANES, tq), jnp.inf, F32),
                                                  jnp.full((SUBLANES, tq), -jnp.inf, F32)))
    s_min = jnp.min(lo8, axis=0, keepdims=True)
    s_max = jnp.max(hi8, axis=0, keepdims=True)

    def count_ge(cand):
        def body(j, acc):
            return acc + fold8(jnp.where(score_sc[j] >= cand, 1.0, 0.0), jnp.sum)
        acc = lax.fori_loop(0, nj, body, jnp.zeros((SUBLANES, tq), F32))
        return acc.sum(axis=0, keepdims=True)

    def bisect_cond(state):
        it, n_open = state[0], state[1]
        return jnp.logical_and(it < BISECT_STEPS, n_open > 0)

    def bisect_step(state):
        it, _, open_q, lo, hi = state
        mid = lo + (hi - lo) * 0.5
        cnt = count_ge(mid)
        ok = cnt >= n_sel
        open_q = jnp.where(cnt == n_sel, 0, open_q)
        lo, hi = jnp.where(ok, mid, lo), jnp.where(ok, hi, mid)
        return it + 1, jnp.sum(open_q), open_q, lo, hi

    hi0 = s_max + jnp.abs(s_max) * 2.0 ** -20 + 1e-30
    state = (jnp.int32(0), jnp.int32(1), jnp.ones((1, tq), I32), s_min, hi0)
    thr_lo, thr_hi = lax.while_loop(bisect_cond, bisect_step, state)[3:]
    need_eq = n_sel - count_ge(thr_hi)

    for h in range(N_HEADS):
        qz_sc[h] = jnp.concatenate([qt_ref[h * HEAD_DIM:(h + 1) * HEAD_DIM, :], slope_ref[h]], axis=0)
    acc_sc[...] = jnp.zeros_like(acc_sc)
    m_sc[...] = jnp.full(m_sc.shape, NEG_BIG, F32)
    stril = (row > col).astype(BF16)
    n_chunk = 4
    rows_c = tk // n_chunk

    def attend_block(j, eq_seen):
        sb = score_sc[j]
        above = sb >= thr_hi
        eq = jnp.logical_and(sb >= thr_lo, jnp.logical_not(above))
        eq_f = jnp.where(eq, 1.0, 0.0).astype(BF16)
        before = jnp.dot(stril, eq_f, preferred_element_type=F32) + eq_seen
        sel = jnp.logical_or(above, jnp.logical_and(eq, before < need_eq))
        bias_sc[...] = jnp.where(sel, 0.0, -jnp.inf)
        block_max = []
        for h in range(N_HEADS):
            lg = jnp.dot(k_ref[j, :, h * LANES:(h + 1) * LANES], qz_sc[h],
                         preferred_element_type=F32) + bias_sc[...]
            lg_sc[h] = lg
            block_max.append(jnp.max(lg, axis=0, keepdims=True))
        m_old = m_sc[...]
        m_new = jnp.maximum(m_old, jnp.concatenate(block_max, axis=0))
        alpha = jnp.exp2(m_old - m_new)
        m_sc[...] = m_new
        for h in range(N_HEADS):
            for c in range(n_chunk):
                rs = slice(c * rows_c, (c + 1) * rows_c)
                p_sc[h, rs, :] = jnp.exp2(lg_sc[h, rs, :] - m_new[h:h + 1, :]).astype(BF16)
            pv = jnp.dot(vt_ref[j, h * ATT_VROWS:(h + 1) * ATT_VROWS, :], p_sc[h],
                         preferred_element_type=F32)
            sl = slice(h * ATT_VROWS, (h + 1) * ATT_VROWS)
            acc_sc[sl, :] = alpha[h:h + 1, :] * acc_sc[sl, :] + pv
        return eq_seen + jnp.sum(eq_f.astype(F32), axis=0, keepdims=True)

    lax.fori_loop(0, nj, attend_block, jnp.zeros((1, tq), F32))
    for h in range(N_HEADS):
        base = h * ATT_VROWS
        o_ref[h * HEAD_DIM:(h + 1) * HEAD_DIM, :] = (
            acc_sc[base:base + HEAD_DIM, :] / acc_sc[base + HEAD_DIM:base + HEAD_DIM + 1, :]).astype(o_ref.dtype)


def _alibi_rows(tq):
    rows = []
    for h in range(N_HEADS):
        rest = 2.0 ** (-8.0 * (h + 1) / N_HEADS) * Q_SCALE * HEAD_DIM ** 0.5
        pieces = []
        for _ in range(ALIBI_SPLIT):
            piece = float(np.asarray(rest, dtype=BF16).astype(np.float32))
            pieces.append(piece)
            rest -= piece
        col = pieces + [ATT_TK * x for x in pieces] + [0.0] * (HEAD_DIM - 2 * ALIBI_SPLIT)
        rows.append(col)
    return jnp.broadcast_to(jnp.asarray(np.asarray(rows, np.float32))[:, :, None],
                            (N_HEADS, HEAD_DIM, tq)).astype(BF16)


def _dsa(q, k, v, qi, ki, wi, n_batch, seq):
    tq, tk = ATT_TQ, ATT_TK
    nkb = seq // tk
    n_sel = min(TOPK_KEYS, seq // 4)
    to_t = lambda a: a.reshape(n_batch, seq, a.shape[-1]).transpose(0, 2, 1)
    qt, qit, wt = to_t(q), to_t(qi), to_t(wi)
    kb = k.reshape(n_batch, nkb, tk, N_HEADS * LANES)
    kib = ki.reshape(n_batch, nkb, tk, IDX_DIM)
    vt = v.reshape(n_batch, nkb, tk, N_HEADS, HEAD_DIM).transpose(0, 1, 3, 4, 2)
    extra = jnp.zeros((ATT_VROWS - HEAD_DIM, 1), BF16).at[0, 0].set(1.0)
    vt = jnp.concatenate([vt, jnp.broadcast_to(extra, vt.shape[:3] + (ATT_VROWS - HEAD_DIM, tk))], axis=3)
    vt = vt.reshape(n_batch, nkb, N_HEADS * ATT_VROWS, tk)
    per_batch = lambda shape: pl.BlockSpec((None,) + shape, lambda b, i: (b, 0, 0, 0),
                                           pipeline_mode=pl.Buffered(1))
    per_q = lambda r: pl.BlockSpec((None, r, tq), lambda b, i: (b, 0, i))
    out = pl.pallas_call(
        functools.partial(_dsa_kernel, n_sel=n_sel),
        grid=(n_batch, seq // tq),
        in_specs=[per_batch((nkb, tk, IDX_DIM)), per_q(IDX_HEADS * IDX_DIM), per_q(IDX_HEADS),
                  per_batch((nkb, tk, N_HEADS * LANES)), per_q(ATTN_WIDTH),
                  per_batch((nkb, N_HEADS * ATT_VROWS, tk)),
                  pl.BlockSpec((N_HEADS, HEAD_DIM, tq), lambda b, i: (0, 0, 0))],
        out_specs=per_q(ATTN_WIDTH),
        out_shape=jax.ShapeDtypeStruct((n_batch, ATTN_WIDTH, seq), BF16),
        scratch_shapes=[pltpu.VMEM((nkb, tk, tq), F32), pltpu.VMEM((N_HEADS, LANES, tq), BF16),
                        pltpu.VMEM((N_HEADS * ATT_VROWS, tq), F32), pltpu.VMEM((N_HEADS, tq), F32),
                        pltpu.VMEM((tk, tq), F32), pltpu.VMEM((N_HEADS, tk, tq), F32),
                        pltpu.VMEM((N_HEADS, tk, tq), BF16)],
        compiler_params=_cparams(("parallel", "arbitrary")),
        name="dsa",
    )(kib, qit, wt, kb, qt, vt, _alibi_rows(tq))
    return out.transpose(0, 2, 1).reshape(n_batch * seq, ATTN_WIDTH)


def _merge_kernel(x_ref, g0_ref, b0_ref, ys_ref, u_ref, dsk_ref, att_ref, gp_ref, bg_ref,
                  wglu_ref, wap_ref, wo_ref, g1_ref, b1_ref, wr_ref, br_ref,
                  h1_ref, idx_ref, gate_ref):
    d = D_MODEL
    h = _layer_norm(x_ref[...], g0_ref[...], b0_ref[...])
    y = jax.nn.gelu(ys_ref[...] + dsk_ref[...] * u_ref[...])
    glu = jnp.dot(y.astype(BF16), wglu_ref[...], preferred_element_type=F32)
    y_ssm = glu[:, :d] * jax.nn.sigmoid(glu[:, d:])
    y_att = jnp.dot(att_ref[...], wap_ref[...], preferred_element_type=F32)
    gs = jax.nn.sigmoid(gp_ref[...].astype(F32) + bg_ref[...])
    mixed = gs[:, :d] * y_ssm + gs[:, d:] * y_att
    mix = jnp.dot(mixed.astype(BF16), wo_ref[...], preferred_element_type=F32)
    h1 = _layer_norm(DEEPNORM_ALPHA * h + mix, g1_ref[...], b1_ref[...])
    h1_ref[...] = h1
    logits = jnp.dot(h1, wr_ref[...], precision=lax.Precision.HIGHEST,
                     preferred_element_type=F32) + br_ref[...]
    lane = lax.broadcasted_iota(I32, logits.shape, 1)
    logits = jnp.where(lane < N_EXPERTS, logits, -jnp.inf)
    idx_out = jnp.zeros(logits.shape, I32)
    val_out = jnp.zeros(logits.shape, F32)
    vals = []
    for kk in range(TOP_K):
        vmax = jnp.max(logits, axis=-1, keepdims=True)
        imax = jnp.min(jnp.where(logits == vmax, lane, LANES), axis=-1, keepdims=True)
        idx_out = jnp.where(lane == kk, imax, idx_out)
        vals.append(vmax)
        logits = jnp.where(lane == imax, -jnp.inf, logits)
    es = [jnp.exp(vk - vals[0]) for vk in vals]
    den = es[0] + es[1] + es[2] + es[3]
    for kk in range(TOP_K):
        val_out = jnp.where(lane == kk, es[kk] / den, val_out)
    idx_ref[...] = idx_out
    gate_ref[...] = val_out


def _merge(xt, ln0_g, ln0_b, ys, u, d_skip, att, gp, b_gate, w_glu, w_ap, w_o, ln1_g, ln1_b,
           w_router, b_router):
    t, d = xt.shape
    tm = 256
    wr = jnp.pad(w_router, ((0, 0), (0, LANES - N_EXPERTS)))
    br = jnp.pad(b_router, (0, LANES - N_EXPERTS)).reshape(1, LANES)
    ins = [xt, ln0_g.reshape(1, d), ln0_b.reshape(1, d), ys, u, d_skip.reshape(1, SSM_WIDTH), att, gp,
           b_gate.reshape(1, 2 * d), w_glu.astype(BF16), w_ap.astype(BF16), w_o.astype(BF16),
           ln1_g.reshape(1, d), ln1_b.reshape(1, d), wr, br]
    tiled = {0, 3, 4, 6, 7}
    in_specs = []
    for n, a in enumerate(ins):
        if n in tiled:
            in_specs.append(pl.BlockSpec((tm, a.shape[1]), lambda i: (i, 0)))
        else:
            in_specs.append(pl.BlockSpec(a.shape, lambda i: (0, 0)))
    return pl.pallas_call(
        _merge_kernel,
        grid=(t // tm,),
        in_specs=in_specs,
        out_specs=[pl.BlockSpec((tm, d), lambda i: (i, 0)), pl.BlockSpec((tm, LANES), lambda i: (i, 0)),
                   pl.BlockSpec((tm, LANES), lambda i: (i, 0))],
        out_shape=[jax.ShapeDtypeStruct((t, d), F32), jax.ShapeDtypeStruct((t, LANES), I32),
                   jax.ShapeDtypeStruct((t, LANES), F32)],
        compiler_params=_cparams(("parallel",)),
        name="merge",
    )(*ins)


def _rank_kernel(idx_ref, rank_ref, cnt_ref, run_sc):
    tm = idx_ref.shape[0]

    @pl.when(pl.program_id(0) == 0)
    def _():
        run_sc[...] = jnp.zeros_like(run_sc)

    idx = idx_ref[...]
    lane = lax.broadcasted_iota(I32, (tm, LANES), 1)
    hots = [lane == idx[:, kk:kk + 1] for kk in range(TOP_K)]
    multi = sum(hh.astype(F32) for hh in hots)
    r_i = lax.broadcasted_iota(I32, (tm, tm), 0)
    c_i = lax.broadcasted_iota(I32, (tm, tm), 1)
    before = jnp.dot((c_i < r_i).astype(BF16), multi.astype(BF16), preferred_element_type=F32)
    base = before + run_sc[...]
    out = jnp.zeros((tm, LANES), I32)
    for kk in range(TOP_K):
        rk = jnp.sum(jnp.where(hots[kk], base, 0.0), axis=-1, keepdims=True)
        out = jnp.where(lane == kk, rk.astype(I32), out)
    rank_ref[...] = out
    run_sc[...] = run_sc[...] + jnp.sum(multi, axis=0, keepdims=True)
    cnt_ref[...] = run_sc[...]


def _rank(idx):
    t = idx.shape[0]
    tm = 512
    return pl.pallas_call(
        _rank_kernel,
        grid=(t // tm,),
        in_specs=[pl.BlockSpec((tm, LANES), lambda i: (i, 0))],
        out_specs=[pl.BlockSpec((tm, LANES), lambda i: (i, 0)), pl.BlockSpec((1, LANES), lambda i: (0, 0))],
        out_shape=[jax.ShapeDtypeStruct((t, LANES), I32), jax.ShapeDtypeStruct((1, LANES), F32)],
        scratch_shapes=[pltpu.VMEM((1, LANES), F32)],
        compiler_params=_cparams(("arbitrary",)),
        name="moe_rank",
    )(idx)


def _row_copy(src_ref, src_row, dst_ref, dst_row, sem):
    return pltpu.make_async_copy(src_ref.at[pl.ds(src_row, 1), :], dst_ref.at[pl.ds(dst_row, 1), :], sem)


def _dispatch_kernel(dest_ref, h_ref, xs_in_ref, xs_ref, sem):
    del xs_in_ref
    tm = h_ref.shape[0]
    base = pl.program_id(0) * tm * TOP_K

    def start(r, c):
        for kk in range(TOP_K):
            _row_copy(h_ref, r, xs_ref, dest_ref[base + r * TOP_K + kk], sem).start()
        return c

    lax.fori_loop(0, tm, start, 0)

    def wait(r, c):
        for kk in range(TOP_K):
            _row_copy(h_ref, 0, xs_ref, 0, sem).wait()
        return c

    lax.fori_loop(0, tm, wait, 0)


def _dispatch(dest_flat, h1, n_rows):
    t, d = h1.shape
    tm = 128
    return pl.pallas_call(
        _dispatch_kernel,
        grid_spec=pltpu.PrefetchScalarGridSpec(
            num_scalar_prefetch=1,
            grid=(t // tm,),
            in_specs=[pl.BlockSpec((tm, d), lambda i, dest: (i, 0)), pl.BlockSpec(memory_space=pl.ANY)],
            out_specs=pl.BlockSpec(memory_space=pl.ANY),
            scratch_shapes=[pltpu.SemaphoreType.DMA(())]),
        out_shape=jax.ShapeDtypeStruct((n_rows, d), F32),
        input_output_aliases={2: 0},
        compiler_params=_cparams(("arbitrary",)),
        name="moe_dispatch",
    )(dest_flat, h1, jnp.zeros((n_rows, d), F32))


def _ffn_kernel(be_ref, nused_ref, xs_ref, wup_ref, bup_ref, wdn_ref, bdn_ref, y_ref):
    b = pl.program_id(0)

    @pl.when(b < nused_ref[0])
    def _():
        x = xs_ref[...].astype(BF16)
        hdn = jnp.dot(x, wup_ref[...], preferred_element_type=F32) + bup_ref[...]
        h_gate = jnp.minimum(hdn[:, :D_FF], SWIGLU_LIMIT)
        h_lin = jnp.clip(hdn[:, D_FF:], -SWIGLU_LIMIT, SWIGLU_LIMIT)
        act = (h_lin + 1.0) * (h_gate * jax.nn.sigmoid(SWIGLU_ALPHA * h_gate))
        y_ref[...] = jnp.dot(act.astype(BF16), wdn_ref[...], preferred_element_type=F32) + bdn_ref[...]

    @pl.when(b >= nused_ref[0])
    def _():
        y_ref[...] = jnp.zeros_like(y_ref)


def _ffn(blk_expert, n_used, xs, w_up, b_up, w_down, b_down):
    n_rows, d = xs.shape
    r = MOE_ROWS
    e = N_EXPERTS
    return pl.pallas_call(
        _ffn_kernel,
        grid_spec=pltpu.PrefetchScalarGridSpec(
            num_scalar_prefetch=2,
            grid=(n_rows // r,),
            in_specs=[pl.BlockSpec((r, d), lambda b, be, nu: (b, 0)),
                      pl.BlockSpec((None, d, 2 * D_FF), lambda b, be, nu: (be[b], 0, 0)),
                      pl.BlockSpec((None, 1, 2 * D_FF), lambda b, be, nu: (be[b], 0, 0)),
                      pl.BlockSpec((None, D_FF, d), lambda b, be, nu: (be[b], 0, 0)),
                      pl.BlockSpec((None, 1, d), lambda b, be, nu: (be[b], 0, 0))],
            out_specs=pl.BlockSpec((r, d), lambda b, be, nu: (b, 0))),
        out_shape=jax.ShapeDtypeStruct((n_rows, d), F32),
        compiler_params=_cparams(("arbitrary",)),
        name="moe_ffn",
    )(blk_expert, n_used, xs, w_up.astype(BF16), b_up.reshape(e, 1, 2 * D_FF),
      w_down.astype(BF16), b_down.reshape(e, 1, d))


def _combine_kernel(dest_ref, h1_ref, gate_ref, g2_ref, b2_ref, y_ref, o_ref, buf, sem):
    tm = h1_ref.shape[0]
    base = pl.program_id(0) * tm * TOP_K

    def start(r, c):
        for kk in range(TOP_K):
            _row_copy(y_ref, dest_ref[base + r * TOP_K + kk], buf.at[kk], r, sem).start()
        return c

    lax.fori_loop(0, tm, start, 0)

    def wait(r, c):
        for kk in range(TOP_K):
            _row_copy(y_ref, 0, buf.at[kk], 0, sem).wait()
        return c

    lax.fori_loop(0, tm, wait, 0)
    gates = gate_ref[...]
    ffn = jnp.zeros(h1_ref.shape, F32)
    for kk in range(TOP_K):
        ffn = ffn + buf[kk] * gates[:, kk:kk + 1]
    o_ref[...] = _layer_norm(DEEPNORM_ALPHA * h1_ref[...] + ffn, g2_ref[...], b2_ref[...])


def _combine(dest_flat, h1, gates, ln2_g, ln2_b, y):
    t, d = h1.shape
    tm = 128
    return pl.pallas_call(
        _combine_kernel,
        grid_spec=pltpu.PrefetchScalarGridSpec(
            num_scalar_prefetch=1,
            grid=(t // tm,),
            in_specs=[pl.BlockSpec((tm, d), lambda i, dest: (i, 0)),
                      pl.BlockSpec((tm, LANES), lambda i, dest: (i, 0)),
                      pl.BlockSpec((1, d), lambda i, dest: (0, 0)),
                      pl.BlockSpec((1, d), lambda i, dest: (0, 0)),
                      pl.BlockSpec(memory_space=pl.ANY)],
            out_specs=pl.BlockSpec((tm, d), lambda i, dest: (i, 0)),
            scratch_shapes=[pltpu.VMEM((TOP_K, tm, d), F32), pltpu.SemaphoreType.DMA(())]),
        out_shape=jax.ShapeDtypeStruct((t, d), F32),
        compiler_params=_cparams(("arbitrary",)),
        name="moe_combine",
    )(dest_flat, h1, gates, ln2_g.reshape(1, d), ln2_b.reshape(1, d), y)


def kernel(x, ln_in_g, ln_in_b, w_in, b_gate, lam_re, lam_im, log_dt, b_re, b_im, c_re, c_im, d_skip,
           w_glu, w_attn_proj, w_o, ln1_g, ln1_b, w_router, b_router, w_up, b_up, w_down, b_down,
           ln2_g, ln2_b):
    n_batch, seq, d = x.shape
    assert d == D_MODEL and seq % ATT_TQ == 0 and w_in.shape[0] == DEPTH
    t = n_batch * seq
    xt = x.reshape(t, d)

    u, q, k, v, qi, sm, gp = _inproj(xt, ln_in_g, ln_in_b, w_in[0], seq)
    ki = sm[:, :IDX_DIM].astype(BF16)
    wi = sm[:, IDX_DIM:IDX_DIM + IDX_HEADS]

    ys = _ssm(u, n_batch, seq, lam_re[0], lam_im[0], log_dt[0], b_re[0], b_im[0], c_re[0], c_im[0])
    att = _dsa(q, k, v, qi, ki, wi, n_batch, seq)
    h1, idx, gates = _merge(xt, ln_in_g, ln_in_b, ys, u, d_skip[0], att, gp, b_gate[0], w_glu[0],
                            w_attn_proj[0], w_o[0], ln1_g[0], ln1_b[0], w_router[0], b_router[0])

    rank, counts = _rank(idx)
    counts = counts[0, :N_EXPERTS].astype(I32)
    padded = (counts + MOE_ROWS - 1) // MOE_ROWS * MOE_ROWS
    pends = jnp.cumsum(padded)
    pstarts = pends - padded
    top_idx = idx[:, :TOP_K]
    dest = (pstarts[top_idx] + rank[:, :TOP_K]).reshape(-1)
    n_rows = t * TOP_K + N_EXPERTS * MOE_ROWS
    blk_start = jnp.arange(n_rows // MOE_ROWS, dtype=I32) * MOE_ROWS
    blk_expert = jnp.minimum(jnp.sum((pends[None, :] <= blk_start[:, None]).astype(I32), axis=1),
                             N_EXPERTS - 1)
    n_used = (pends[-1:] // MOE_ROWS).astype(I32)

    xs = _dispatch(dest, h1, n_rows)
    y = _ffn(blk_expert, n_used, xs, w_up[0], b_up[0], w_down[0], b_down[0])
    out = _combine(dest, h1, gates, ln2_g[0], ln2_b[0], y)
    return out.reshape(n_batch, seq, d)
```

```python
import functools
import math

import jax
import jax.numpy as jnp
import numpy as np
from jax import lax
from jax.experimental import pallas as pl
from jax.experimental.pallas import tpu as pltpu

F32 = jnp.float32
BF16 = jnp.bfloat16
I32 = jnp.int32
I16 = jnp.int16

D_MODEL = 1024
SSM_WIDTH = 512
SSM_P = 16
SSM_G = 32
SSM_N = 64
SSM_CHUNK = 16
N_HEADS = 8
HEAD_DIM = 64
ATTN_WIDTH = 512
IDX_HEADS = 8
IDX_DIM = 64
TOPK_KEYS = 256
N_EXPERTS = 32
TOP_K = 4
D_FF = 1024
SWIGLU_LIMIT = 7.0
SWIGLU_ALPHA = 1.702
LN_EPS = 1e-5
DEPTH = 1
DEEPNORM_ALPHA = (2 * DEPTH) ** 0.25

LANES = 128
SUBLANES = 8
ATT_TQ = 256
ATT_TK = 256
ATT_VROWS = HEAD_DIM + 16
ALIBI_SPLIT = 4
MOE_ROWS = 256
BISECT_STEPS = 32
NEG_BIG = -1e30
Q_SCALE = float(np.float32(HEAD_DIM ** -0.5 * math.log2(math.e)))
VMEM_LIMIT = 56 * 1024 * 1024


def _cparams(sem, vmem=VMEM_LIMIT):
    return pltpu.CompilerParams(dimension_semantics=sem, vmem_limit_bytes=vmem)


def _layer_norm(x, g, b):
    mu = jnp.mean(x, axis=-1, keepdims=True)
    xc = x - mu
    var = jnp.mean(xc * xc, axis=-1, keepdims=True)
    return xc * lax.rsqrt(var + LN_EPS) * g + b


def _inproj_kernel(x_ref, g_ref, b_ref, wu_ref, wqt_ref, wk_ref, wvt_ref, wqit_ref, wsm_ref, wsmt_ref, wg_ref,
                   ones_ref, u_ref, qt_ref, k_ref, vt_ref, qit_ref, ki_ref, wt_ref, gp_ref, *, seq):
    h = _layer_norm(x_ref[...], g_ref[...], b_ref[...])
    hb = h.astype(BF16)
    tm = x_ref.shape[0]

    def proj(w_ref):
        return jnp.dot(hb, w_ref[...], preferred_element_type=F32)

    def proj_t(w_ref):
        return lax.dot_general(w_ref[...], hb, (((1,), (1,)), ((), ())), preferred_element_type=F32)

    u_ref[...] = proj(wu_ref)
    qt_ref[...] = (proj_t(wqt_ref) * Q_SCALE).astype(BF16)
    shape = (tm, N_HEADS * LANES)
    pos = (pl.program_id(0) * tm + lax.broadcasted_iota(I32, shape, 0)) % seq
    sub = lax.broadcasted_iota(I32, shape, 1) % LANES - HEAD_DIM
    feat = jnp.where(jnp.logical_and(sub >= 0, sub < ALIBI_SPLIT), pos % ATT_TK,
                     jnp.where(jnp.logical_and(sub >= ALIBI_SPLIT, sub < 2 * ALIBI_SPLIT), pos // ATT_TK, 0))
    k_ref[...] = (proj(wk_ref) + feat.astype(F32)).astype(BF16)
    vt = (proj_t(wvt_ref) + ones_ref[...]).astype(BF16)
    for kb in range(tm // ATT_TK):
        vt_ref[kb] = vt[:, kb * ATT_TK:(kb + 1) * ATT_TK]
    qit_ref[...] = proj_t(wqit_ref).astype(BF16)
    ki_ref[...] = proj(wsm_ref)[:, :IDX_DIM].astype(BF16)
    wt_ref[...] = proj_t(wsmt_ref)[IDX_DIM:IDX_DIM + IDX_HEADS, :] * (IDX_DIM ** -0.5 * IDX_HEADS ** -0.5)
    gp_ref[...] = proj(wg_ref).astype(BF16)


def _inproj(xt, ln_g, ln_b, w_in, n_batch, seq):
    t, d = xt.shape
    tm = 512
    n_l = seq // tm
    o = 0
    ws = []
    for n in (SSM_WIDTH, ATTN_WIDTH, ATTN_WIDTH, ATTN_WIDTH, IDX_HEADS * IDX_DIM):
        ws.append(w_in[:, o:o + n].astype(BF16))
        o += n
    w_u, w_q, w_k, w_v, w_qi = ws
    w_k = jnp.pad(w_k.reshape(d, N_HEADS, HEAD_DIM),
                  ((0, 0), (0, 0), (0, LANES - HEAD_DIM))).reshape(d, N_HEADS * LANES)
    w_vt = jnp.pad(w_v.T.reshape(N_HEADS, HEAD_DIM, d),
                   ((0, 0), (0, ATT_VROWS - HEAD_DIM), (0, 0))).reshape(N_HEADS * ATT_VROWS, d)
    ones_col = jnp.zeros((N_HEADS, ATT_VROWS, 1), F32).at[:, HEAD_DIM, 0].set(1.0).reshape(-1, 1)
    n_small = IDX_DIM + IDX_HEADS
    w_small = jnp.pad(w_in[:, o:o + n_small], ((0, 0), (0, LANES - n_small))).astype(BF16)
    o += n_small
    w_gate = w_in[:, o:].astype(BF16)
    full = lambda a: pl.BlockSpec(a.shape, lambda i: (0,) * a.ndim)
    row = lambda n: pl.BlockSpec((tm, n), lambda i: (i, 0))
    col = lambda r: pl.BlockSpec((None, r, tm), lambda i: (i // n_l, 0, i % n_l))
    ins = [xt, ln_g.reshape(1, d), ln_b.reshape(1, d), w_u, w_q.T, w_k, w_vt, w_qi.T, w_small, w_small.T,
           w_gate, ones_col]
    nkb = seq // ATT_TK
    out_specs = [row(SSM_WIDTH), col(ATTN_WIDTH), row(N_HEADS * LANES),
                 pl.BlockSpec((None, tm // ATT_TK, N_HEADS * ATT_VROWS, ATT_TK),
                              lambda i: (i // n_l, i % n_l, 0, 0)),
                 col(IDX_HEADS * IDX_DIM), row(IDX_DIM), col(IDX_HEADS), row(2 * d)]
    out_shape = [jax.ShapeDtypeStruct((t, SSM_WIDTH), F32),
                 jax.ShapeDtypeStruct((n_batch, ATTN_WIDTH, seq), BF16),
                 jax.ShapeDtypeStruct((t, N_HEADS * LANES), BF16),
                 jax.ShapeDtypeStruct((n_batch, nkb, N_HEADS * ATT_VROWS, ATT_TK), BF16),
                 jax.ShapeDtypeStruct((n_batch, IDX_HEADS * IDX_DIM, seq), BF16),
                 jax.ShapeDtypeStruct((t, IDX_DIM), BF16),
                 jax.ShapeDtypeStruct((n_batch, IDX_HEADS, seq), F32),
                 jax.ShapeDtypeStruct((t, 2 * d), BF16)]
    return pl.pallas_call(
        functools.partial(_inproj_kernel, seq=seq),
        grid=(t // tm,),
        in_specs=[row(d)] + [full(a) for a in ins[1:]],
        out_specs=out_specs,
        out_shape=out_shape,
        compiler_params=_cparams(("parallel",)),
        name="inproj",
    )(*ins)


def _ssm_kernel(u_ref, lrr_ref, lir_ref, lrc_ref, lic_ref, ldt_ref, btr_ref, bti_ref, ctr_ref, cti_ref,
                y_ref, tg_sc, sin_sc, buf_a, buf_b, *, n_batch, n_chunk, pad):
    cc, p, n = SSM_CHUNK, SSM_P, SSM_N
    hi = lax.Precision.HIGHEST
    dt = jnp.exp(ldt_ref[...])
    ar, ai = lrr_ref[...] * dt, lir_ref[...] * dt
    mag = jnp.exp(ar)
    a_re, a_im = mag * jnp.cos(ai), mag * jnp.sin(ai)
    lr, li = lrr_ref[...], lir_ref[...]
    den = lr * lr + li * li
    f_re = ((a_re - 1.0) * lr + a_im * li) / den
    f_im = (a_im * lr - (a_re - 1.0) * li) / den
    bb_re = f_re * btr_ref[...] - f_im * bti_ref[...]
    bb_im = f_re * bti_ref[...] + f_im * btr_ref[...]
    arc, aic = lrc_ref[...] * dt, lic_ref[...] * dt
    tau = (lax.broadcasted_iota(I32, (1, cc * p), 1) // p).astype(F32)
    rmag = jnp.exp(arc * tau)
    pw_re, pw_im = rmag * jnp.cos(aic * tau), rmag * jnp.sin(aic * tau)
    r_re = pw_re * ctr_ref[...] - pw_im * cti_ref[...]
    r_im = pw_re * cti_ref[...] + pw_im * ctr_ref[...]
    m0 = (jnp.dot(bb_re, r_re, precision=hi, preferred_element_type=F32)
          - jnp.dot(bb_im, r_im, precision=hi, preferred_element_type=F32))
    lane = lax.broadcasted_iota(I32, (p, cc * p), 1)
    e_pow = (cc - 1 - lax.broadcasted_iota(I32, (cc, 1), 0)).astype(F32)
    pmag = jnp.exp(ar * e_pow)
    ap_re, ap_im = pmag * jnp.cos(ai * e_pow), pmag * jnp.sin(ai * e_pow)
    for s in range(cc):
        blk = m0 if s == 0 else jnp.where(lane >= s * p, pltpu.roll(m0, s * p, axis=1), 0.0)
        tg_sc[s * p:(s + 1) * p, :] = blk.astype(BF16)
        pr, pi = ap_re[s:s + 1, :], ap_im[s:s + 1, :]
        sin_sc[s * p:(s + 1) * p, :] = jnp.concatenate(
            [bb_re * pr - bb_im * pi, bb_re * pi + bb_im * pr], axis=1).astype(BF16)
    mc = jnp.exp(arc)
    ac_re, ac_im = mc * jnp.cos(aic), mc * jnp.sin(aic)
    r1_re = ac_re * r_re - ac_im * r_im
    r1_im = ac_re * r_im + ac_im * r_re
    cout = jnp.concatenate([r1_re, -r1_im], axis=0).astype(BF16)

    u = u_ref[...]
    y_intra = jnp.dot(u, tg_sc[...], preferred_element_type=F32)
    x_loc = jnp.dot(u, sin_sc[...], preferred_element_type=F32)

    seg = pad + n_chunk

    @pl.when(pl.program_id(0) == 0)
    def _():
        for b in range(n_batch):
            buf_a[b * seg:b * seg + pad, :] = jnp.zeros((pad, 2 * n), F32)
            buf_b[b * seg:b * seg + pad, :] = jnp.zeros((pad, 2 * n), F32)

    for b in range(n_batch):
        buf_a[b * seg + pad:(b + 1) * seg, :] = x_loc[b * n_chunk:(b + 1) * n_chunk, :]
    cm = jnp.exp(ar * cc)
    m_re, m_im = cm * jnp.cos(ai * cc), cm * jnp.sin(ai * cc)
    src, dst = buf_a, buf_b
    shift = 1
    while shift < n_chunk:
        m_a = jnp.concatenate([m_re, m_re], axis=1)
        m_b = jnp.concatenate([-m_im, m_im], axis=1)
        for b in range(n_batch):
            base = b * seg + pad
            cur = src[base:base + n_chunk, :]
            prev = src[base - shift:base - shift + n_chunk, :]
            dst[base:base + n_chunk, :] = cur + prev * m_a + pltpu.roll(prev, n, axis=1) * m_b
        m_re, m_im = m_re * m_re - m_im * m_im, 2.0 * m_re * m_im
        src, dst = dst, src
        shift *= 2
    x_in = jnp.concatenate(
        [src[b * seg + pad - 1:b * seg + pad - 1 + n_chunk, :] for b in range(n_batch)], axis=0)
    y_ref[...] = y_intra + jnp.dot(x_in.astype(BF16), cout, preferred_element_type=F32)


def _ssm(u, n_batch, seq, lam_re, lam_im, log_dt, b_re, b_im, c_re, c_im):
    g, p, n, cc = SSM_G, SSM_P, SSM_N, SSM_CHUNK
    n_chunk = seq // cc
    rows = n_batch * n_chunk
    pad = max(n_chunk // 2, SUBLANES)
    ug = (u.astype(BF16).reshape(n_batch, n_chunk, cc, g, p)
          .transpose(3, 0, 1, 2, 4).reshape(g, rows, cc * p))
    ct_re = jnp.tile(c_re.transpose(0, 2, 1), (1, 1, cc))
    ct_im = jnp.tile(c_im.transpose(0, 2, 1), (1, 1, cc))
    ins = [ug, lam_re.reshape(g, 1, n), lam_im.reshape(g, 1, n), lam_re.reshape(g, n, 1),
           lam_im.reshape(g, n, 1), log_dt.reshape(g, 1, 1), b_re.transpose(0, 2, 1),
           b_im.transpose(0, 2, 1), ct_re, ct_im]
    per_group = lambda a: pl.BlockSpec((None,) + a.shape[1:], lambda i: (i, 0, 0))
    y = pl.pallas_call(
        functools.partial(_ssm_kernel, n_batch=n_batch, n_chunk=n_chunk, pad=pad),
        grid=(g,),
        in_specs=[per_group(a) for a in ins],
        out_specs=pl.BlockSpec((None, rows, cc * p), lambda i: (i, 0, 0)),
        out_shape=jax.ShapeDtypeStruct((g, rows, cc * p), F32),
        scratch_shapes=[pltpu.VMEM((cc * p, cc * p), BF16), pltpu.VMEM((cc * p, 2 * n), BF16),
                        pltpu.VMEM((n_batch * (pad + n_chunk), 2 * n), F32),
                        pltpu.VMEM((n_batch * (pad + n_chunk), 2 * n), F32)],
        compiler_params=_cparams(("arbitrary",)),
        name="ssm",
    )(*ins)
    return (y.reshape(g, n_batch, n_chunk, cc, p).transpose(1, 2, 3, 0, 4)
            .reshape(n_batch * seq, g * p))


def _dsa_kernel(ki_ref, qit_ref, wt_ref, k_ref, qt_ref, vt_ref, slope_ref, o_ref,
                score_sc, qz_sc, acc_sc, m_sc, bias_sc, lg_sc, p_sc, *, n_sel):
    tq, tk = ATT_TQ, ATT_TK
    i = pl.program_id(1)
    nj = i + 1
    row = lax.broadcasted_iota(I32, (tk, tq), 0)
    col = lax.broadcasted_iota(I32, (tk, tq), 1)
    causal_diag = row <= col

    def fold8(x, op):
        return op(x.reshape(tk // SUBLANES, SUBLANES, tq), axis=0)

    def score_block(j, carry):
        lo8, hi8 = carry
        kib = ki_ref[j]
        s = jnp.zeros((tk, tq), F32)
        for h in range(IDX_HEADS):
            d = jnp.dot(kib, qit_ref[h * IDX_DIM:(h + 1) * IDX_DIM, :], preferred_element_type=F32)
            s = s + jnp.maximum(d, 0.0) * wt_ref[h:h + 1, :]
        causal = jnp.logical_or(j < i, causal_diag)
        score_sc[j] = jnp.where(causal, s, -jnp.inf)
        return (jnp.minimum(lo8, fold8(jnp.where(causal, s, jnp.inf), jnp.min)),
                jnp.maximum(hi8, fold8(jnp.where(causal, s, -jnp.inf), jnp.max)))

    lo8, hi8 = lax.fori_loop(0, nj, score_block, (jnp.full((SUBLANES, tq), jnp.inf, F32),
                                                  jnp.full((SUBLANES, tq), -jnp.inf, F32)))
    s_min = jnp.min(lo8, axis=0, keepdims=True)
    s_max = jnp.max(hi8, axis=0, keepdims=True)

    def count_ge(cand):
        def body(p, acc):
            j1 = jnp.minimum(2 * p + 1, nj - 1)
            w1 = jnp.where(2 * p + 1 < nj, 1.0, 0.0)
            return (acc + fold8(jnp.where(score_sc[2 * p] >= cand, 1.0, 0.0), jnp.sum)
                    + fold8(jnp.where(score_sc[j1] >= cand, w1, 0.0), jnp.sum))
        acc = lax.fori_loop(0, (nj + 1) // 2, body, jnp.zeros((SUBLANES, tq), F32))
        return acc.sum(axis=0, keepdims=True)

    def bisect_cond(state):
        it, n_open = state[0], state[1]
        return jnp.logical_and(it < BISECT_STEPS, n_open > 0)

    def bisect_step(state):
        it, _, open_q, lo, hi = state
        for _ in range(2):
            mid = lo + (hi - lo) * 0.5
            cnt = count_ge(mid)
            ok = cnt >= n_sel
            open_q = jnp.where(cnt == n_sel, 0, open_q)
            lo, hi = jnp.where(ok, mid, lo), jnp.where(ok, hi, mid)
        return it + 2, jnp.sum(open_q), open_q, lo, hi

    hi0 = s_max + jnp.abs(s_max) * 2.0 ** -20 + 1e-30
    state = (jnp.int32(0), jnp.int32(1), jnp.ones((1, tq), I32), s_min, hi0)
    thr_lo, thr_hi = lax.while_loop(bisect_cond, bisect_step, state)[3:]
    need_eq = n_sel - count_ge(thr_hi)

    for h in range(N_HEADS):
        qz_sc[h] = jnp.concatenate([qt_ref[h * HEAD_DIM:(h + 1) * HEAD_DIM, :], slope_ref[h]], axis=0)
    acc_sc[...] = jnp.zeros_like(acc_sc)
    m_sc[...] = jnp.full(m_sc.shape, NEG_BIG, F32)
    stril = (row > col).astype(BF16)
    n_chunk = 4
    rows_c = tk // n_chunk

    def attend_block(j, eq_seen):
        sb = score_sc[j]
        above = sb >= thr_hi
        eq = jnp.logical_and(sb >= thr_lo, jnp.logical_not(above))
        eq_f = jnp.where(eq, 1.0, 0.0).astype(BF16)
        before = jnp.dot(stril, eq_f, preferred_element_type=F32) + eq_seen
        sel = jnp.logical_or(above, jnp.logical_and(eq, before < need_eq))
        bias_sc[...] = jnp.where(sel, 0.0, -jnp.inf)
        block_max = []
        for h in range(N_HEADS):
            lg = jnp.dot(k_ref[j, :, h * LANES:(h + 1) * LANES], qz_sc[h],
                         preferred_element_type=F32) + bias_sc[...]
            lg_sc[h] = lg
            block_max.append(jnp.max(lg, axis=0, keepdims=True))
        m_old = m_sc[...]
        m_new = jnp.maximum(m_old, jnp.concatenate(block_max, axis=0))
        alpha = jnp.exp2(m_old - m_new)
        m_sc[...] = m_new
        for h in range(N_HEADS):
            for c in range(n_chunk):
                rs = slice(c * rows_c, (c + 1) * rows_c)
                p_sc[h, rs, :] = jnp.exp2(lg_sc[h, rs, :] - m_new[h:h + 1, :]).astype(BF16)
            pv = jnp.dot(vt_ref[j, h * ATT_VROWS:(h + 1) * ATT_VROWS, :], p_sc[h],
                         preferred_element_type=F32)
            sl = slice(h * ATT_VROWS, (h + 1) * ATT_VROWS)
            acc_sc[sl, :] = alpha[h:h + 1, :] * acc_sc[sl, :] + pv
        return eq_seen + jnp.sum(eq_f.astype(F32), axis=0, keepdims=True)

    lax.fori_loop(0, nj, attend_block, jnp.zeros((1, tq), F32))
    for h in range(N_HEADS):
        base = h * ATT_VROWS
        o_ref[h * HEAD_DIM:(h + 1) * HEAD_DIM, :] = (
            acc_sc[base:base + HEAD_DIM, :] / acc_sc[base + HEAD_DIM:base + HEAD_DIM + 1, :]).astype(o_ref.dtype)


def _alibi_rows(tq):
    rows = []
    for h in range(N_HEADS):
        rest = 2.0 ** (-8.0 * (h + 1) / N_HEADS) * Q_SCALE * HEAD_DIM ** 0.5
        pieces = []
        for _ in range(ALIBI_SPLIT):
            piece = float(np.asarray(rest, dtype=BF16).astype(np.float32))
            pieces.append(piece)
            rest -= piece
        col = pieces + [ATT_TK * x for x in pieces] + [0.0] * (HEAD_DIM - 2 * ALIBI_SPLIT)
        rows.append(col)
    return jnp.broadcast_to(jnp.asarray(np.asarray(rows, np.float32))[:, :, None],
                            (N_HEADS, HEAD_DIM, tq)).astype(BF16)


def _dsa(qt, k, vt, qit, ki, wt, n_batch, seq):
    tq, tk = ATT_TQ, ATT_TK
    nkb = seq // tk
    n_sel = min(TOPK_KEYS, seq // 4)
    kb = k.reshape(n_batch, nkb, tk, N_HEADS * LANES)
    kib = ki.reshape(n_batch, nkb, tk, IDX_DIM)
    per_batch = lambda shape: pl.BlockSpec((None,) + shape, lambda b, i: (b, 0, 0, 0),
                                           pipeline_mode=pl.Buffered(1))
    per_q = lambda r: pl.BlockSpec((None, r, tq), lambda b, i: (b, 0, i))
    return pl.pallas_call(
        functools.partial(_dsa_kernel, n_sel=n_sel),
        grid=(n_batch, seq // tq),
        in_specs=[per_batch((nkb, tk, IDX_DIM)), per_q(IDX_HEADS * IDX_DIM), per_q(IDX_HEADS),
                  per_batch((nkb, tk, N_HEADS * LANES)), per_q(ATTN_WIDTH),
                  per_batch((nkb, N_HEADS * ATT_VROWS, tk)),
                  pl.BlockSpec((N_HEADS, HEAD_DIM, tq), lambda b, i: (0, 0, 0))],
        out_specs=per_q(ATTN_WIDTH),
        out_shape=jax.ShapeDtypeStruct((n_batch, ATTN_WIDTH, seq), BF16),
        scratch_shapes=[pltpu.VMEM((nkb, tk, tq), F32), pltpu.VMEM((N_HEADS, LANES, tq), BF16),
                        pltpu.VMEM((N_HEADS * ATT_VROWS, tq), F32), pltpu.VMEM((N_HEADS, tq), F32),
                        pltpu.VMEM((tk, tq), F32), pltpu.VMEM((N_HEADS, tk, tq), F32),
                        pltpu.VMEM((N_HEADS, tk, tq), BF16)],
        compiler_params=_cparams(("parallel", "arbitrary")),
        name="dsa",
    )(kib, qit, wt, kb, qt, vt, _alibi_rows(tq))


def _merge_kernel(x_ref, g0_ref, b0_ref, ys_ref, u_ref, dsk_ref, att_ref, gp_ref, bg_ref,
                  wglu_ref, wap_ref, wo_ref, g1_ref, b1_ref, wr_ref, br_ref,
                  h1_ref, idx_ref, gate_ref):
    d = D_MODEL
    h = _layer_norm(x_ref[...], g0_ref[...], b0_ref[...])
    y = jax.nn.gelu(ys_ref[...] + dsk_ref[...] * u_ref[...])
    glu = jnp.dot(y.astype(BF16), wglu_ref[...], preferred_element_type=F32)
    y_ssm = glu[:, :d] * jax.nn.sigmoid(glu[:, d:])
    y_att = lax.dot_general(att_ref[...], wap_ref[...], (((0,), (0,)), ((), ())),
                            preferred_element_type=F32)
    gs = jax.nn.sigmoid(gp_ref[...].astype(F32) + bg_ref[...])
    mixed = gs[:, :d] * y_ssm + gs[:, d:] * y_att
    mix = jnp.dot(mixed.astype(BF16), wo_ref[...], preferred_element_type=F32)
    h1 = _layer_norm(DEEPNORM_ALPHA * h + mix, g1_ref[...], b1_ref[...])
    h1_ref[...] = h1
    logits = jnp.dot(h1, wr_ref[...], precision=lax.Precision.HIGHEST,
                     preferred_element_type=F32) + br_ref[...]
    lane = lax.broadcasted_iota(I32, logits.shape, 1)
    logits = jnp.where(lane < N_EXPERTS, logits, -jnp.inf)
    idx_out = jnp.zeros(logits.shape, I32)
    val_out = jnp.zeros(logits.shape, F32)
    vals = []
    for kk in range(TOP_K):
        vmax = jnp.max(logits, axis=-1, keepdims=True)
        imax = jnp.min(jnp.where(logits == vmax, lane, LANES), axis=-1, keepdims=True)
        idx_out = jnp.where(lane == kk, imax, idx_out)
        vals.append(vmax)
        logits = jnp.where(lane == imax, -jnp.inf, logits)
    es = [jnp.exp(vk - vals[0]) for vk in vals]
    den = es[0] + es[1] + es[2] + es[3]
    for kk in range(TOP_K):
        val_out = jnp.where(lane == kk, es[kk] / den, val_out)
    idx_ref[...] = idx_out
    gate_ref[...] = val_out


def _merge(xt, ln0_g, ln0_b, ys, u, d_skip, att, gp, b_gate, w_glu, w_ap, w_o, ln1_g, ln1_b,
           w_router, b_router):
    t, d = xt.shape
    tm = 256
    wr = jnp.pad(w_router, ((0, 0), (0, LANES - N_EXPERTS)))
    br = jnp.pad(b_router, (0, LANES - N_EXPERTS)).reshape(1, LANES)
    ins = [xt, ln0_g.reshape(1, d), ln0_b.reshape(1, d), ys, u, d_skip.reshape(1, SSM_WIDTH), att, gp,
           b_gate.reshape(1, 2 * d), w_glu.astype(BF16), w_ap.astype(BF16), w_o.astype(BF16),
           ln1_g.reshape(1, d), ln1_b.reshape(1, d), wr, br]
    tiled = {0, 3, 4, 7}
    n_l = att.shape[2] // tm
    in_specs = []
    for n, a in enumerate(ins):
        if n in tiled:
            in_specs.append(pl.BlockSpec((tm, a.shape[1]), lambda i: (i, 0)))
        elif n == 6:
            in_specs.append(pl.BlockSpec((None, ATTN_WIDTH, tm), lambda i: (i // n_l, 0, i % n_l)))
        else:
            in_specs.append(pl.BlockSpec(a.shape, lambda i: (0, 0)))
    return pl.pallas_call(
        _merge_kernel,
        grid=(t // tm,),
        in_specs=in_specs,
        out_specs=[pl.BlockSpec((tm, d), lambda i: (i, 0)), pl.BlockSpec((tm, LANES), lambda i: (i, 0)),
                   pl.BlockSpec((tm, LANES), lambda i: (i, 0))],
        out_shape=[jax.ShapeDtypeStruct((t, d), F32), jax.ShapeDtypeStruct((t, LANES), I32),
                   jax.ShapeDtypeStruct((t, LANES), F32)],
        compiler_params=_cparams(("parallel",)),
        name="merge",
    )(*ins)


def _rank_kernel(idx_ref, rank_ref, cnt_ref, run_sc):
    tm = idx_ref.shape[0]

    @pl.when(pl.program_id(0) == 0)
    def _():
        run_sc[...] = jnp.zeros_like(run_sc)

    idx = idx_ref[...]
    lane = lax.broadcasted_iota(I32, (tm, LANES), 1)
    hots = [lane == idx[:, kk:kk + 1] for kk in range(TOP_K)]
    multi = sum(hh.astype(F32) for hh in hots)
    r_i = lax.broadcasted_iota(I32, (tm, tm), 0)
    c_i = lax.broadcasted_iota(I32, (tm, tm), 1)
    before = jnp.dot((c_i < r_i).astype(BF16), multi.astype(BF16), preferred_element_type=F32)
    base = before + run_sc[...]
    out = jnp.zeros((tm, LANES), I32)
    for kk in range(TOP_K):
        rk = jnp.sum(jnp.where(hots[kk], base, 0.0), axis=-1, keepdims=True)
        out = jnp.where(lane == kk, rk.astype(I32), out)
    rank_ref[...] = out
    run_sc[...] = run_sc[...] + jnp.sum(multi, axis=0, keepdims=True)
    cnt_ref[...] = run_sc[...]


def _rank(idx):
    t = idx.shape[0]
    tm = 512
    return pl.pallas_call(
        _rank_kernel,
        grid=(t // tm,),
        in_specs=[pl.BlockSpec((tm, LANES), lambda i: (i, 0))],
        out_specs=[pl.BlockSpec((tm, LANES), lambda i: (i, 0)), pl.BlockSpec((1, LANES), lambda i: (0, 0))],
        out_shape=[jax.ShapeDtypeStruct((t, LANES), I32), jax.ShapeDtypeStruct((1, LANES), F32)],
        scratch_shapes=[pltpu.VMEM((1, LANES), F32)],
        compiler_params=_cparams(("arbitrary",)),
        name="moe_rank",
    )(idx)


def _row_copy(src_ref, src_row, dst_ref, dst_row, sem):
    return pltpu.make_async_copy(src_ref.at[pl.ds(src_row, 1), :], dst_ref.at[pl.ds(dst_row, 1), :], sem)


def _dispatch_kernel(dest_ref, pend_ref, h_ref, xs_ref, zero_sc, sem, zsem):
    tm = h_ref.shape[0]
    base = pl.program_id(0) * tm * TOP_K

    @pl.when(pl.program_id(0) == 0)
    def _():
        zero_sc[...] = jnp.zeros_like(zero_sc)

        def zero_copy(e):
            start = pl.multiple_of(pend_ref[e + 1] - MOE_ROWS, MOE_ROWS)
            return pltpu.make_async_copy(zero_sc, xs_ref.at[pl.ds(start, MOE_ROWS), :], zsem)

        for e in range(N_EXPERTS):
            @pl.when(pend_ref[e + 1] > pend_ref[e])
            def _():
                zero_copy(e).start()
        for e in range(N_EXPERTS):
            @pl.when(pend_ref[e + 1] > pend_ref[e])
            def _():
                zero_copy(e).wait()

        def tail_copy(blk):
            return pltpu.make_async_copy(
                zero_sc, xs_ref.at[pl.ds(pl.multiple_of(blk * MOE_ROWS, MOE_ROWS), MOE_ROWS), :], zsem)

        first_free = pend_ref[N_EXPERTS] // MOE_ROWS
        n_blocks = xs_ref.shape[0] // MOE_ROWS
        lax.fori_loop(first_free, n_blocks, lambda blk, c: (tail_copy(blk).start(), c)[1], 0)
        lax.fori_loop(first_free, n_blocks, lambda blk, c: (tail_copy(blk).wait(), c)[1], 0)

    def start(r, c):
        for kk in range(TOP_K):
            _row_copy(h_ref, r, xs_ref, dest_ref[base + r * TOP_K + kk], sem).start()
        return c

    lax.fori_loop(0, tm, start, 0)

    def wait(r, c):
        for kk in range(TOP_K):
            _row_copy(h_ref, 0, xs_ref, 0, sem).wait()
        return c

    lax.fori_loop(0, tm, wait, 0)


def _dispatch(dest_flat, pend0, h1, n_rows):
    t, d = h1.shape
    tm = 128
    return pl.pallas_call(
        _dispatch_kernel,
        grid_spec=pltpu.PrefetchScalarGridSpec(
            num_scalar_prefetch=2,
            grid=(t // tm,),
            in_specs=[pl.BlockSpec((tm, d), lambda i, dest, pend: (i, 0))],
            out_specs=pl.BlockSpec(memory_space=pl.ANY),
            scratch_shapes=[pltpu.VMEM((MOE_ROWS, d), F32), pltpu.SemaphoreType.DMA(()),
                            pltpu.SemaphoreType.DMA(())]),
        out_shape=jax.ShapeDtypeStruct((n_rows, d), F32),
        compiler_params=_cparams(("arbitrary",)),
        name="moe_dispatch",
    )(dest_flat, pend0, h1)


def _ffn_kernel(be_ref, nused_ref, xs_ref, wup_ref, bup_ref, wdn_ref, bdn_ref, y_ref, wup_sc, wdn_sc):
    b = pl.program_id(0)
    used = b < nused_ref[0]

    @pl.when(jnp.logical_and(used, jnp.logical_or(b == 0, be_ref[b] != be_ref[jnp.maximum(b - 1, 0)])))
    def _():
        wup_sc[...] = wup_ref[...].astype(BF16)
        wdn_sc[...] = wdn_ref[...].astype(BF16)

    @pl.when(used)
    def _():
        x = xs_ref[...].astype(BF16)
        hdn = jnp.dot(x, wup_sc[...], preferred_element_type=F32) + bup_ref[...]
        h_gate = jnp.minimum(hdn[:, :D_FF], SWIGLU_LIMIT)
        h_lin = jnp.clip(hdn[:, D_FF:], -SWIGLU_LIMIT, SWIGLU_LIMIT)
        act = (h_lin + 1.0) * (h_gate * jax.nn.sigmoid(SWIGLU_ALPHA * h_gate))
        y_ref[...] = jnp.dot(act.astype(BF16), wdn_sc[...], preferred_element_type=F32) + bdn_ref[...]

    @pl.when(b >= nused_ref[0])
    def _():
        y_ref[...] = jnp.zeros_like(y_ref)


def _ffn(blk_expert, n_used, xs, w_up, b_up, w_down, b_down):
    n_rows, d = xs.shape
    r = MOE_ROWS
    e = N_EXPERTS
    return pl.pallas_call(
        _ffn_kernel,
        grid_spec=pltpu.PrefetchScalarGridSpec(
            num_scalar_prefetch=2,
            grid=(n_rows // r,),
            in_specs=[pl.BlockSpec((r, d), lambda b, be, nu: (jnp.minimum(b, nu[0] - 1), 0)),
                      pl.BlockSpec((None, d, 2 * D_FF), lambda b, be, nu: (be[b], 0, 0)),
                      pl.BlockSpec((None, 1, 2 * D_FF), lambda b, be, nu: (be[b], 0, 0)),
                      pl.BlockSpec((None, D_FF, d), lambda b, be, nu: (be[b], 0, 0)),
                      pl.BlockSpec((None, 1, d), lambda b, be, nu: (be[b], 0, 0))],
            out_specs=pl.BlockSpec((r, d), lambda b, be, nu: (b, 0)),
            scratch_shapes=[pltpu.VMEM((d, 2 * D_FF), BF16), pltpu.VMEM((D_FF, d), BF16)]),
        out_shape=jax.ShapeDtypeStruct((n_rows, d), F32),
        compiler_params=_cparams(("arbitrary",)),
        name="moe_ffn",
    )(blk_expert, n_used, xs, w_up, b_up.reshape(e, 1, 2 * D_FF), w_down, b_down.reshape(e, 1, d))


def _combine_kernel(dest_ref, h1_ref, gate_ref, g2_ref, b2_ref, y_ref, o_ref, buf, sem):
    tm = h1_ref.shape[0]
    base = pl.program_id(0) * tm * TOP_K

    def start(r, c):
        for kk in range(TOP_K):
            _row_copy(y_ref, dest_ref[base + r * TOP_K + kk], buf.at[kk], r, sem).start()
        return c

    lax.fori_loop(0, tm, start, 0)

    def wait(r, c):
        for kk in range(TOP_K):
            _row_copy(y_ref, 0, buf.at[kk], 0, sem).wait()
        return c

    lax.fori_loop(0, tm, wait, 0)
    gates = gate_ref[...]
    ffn = jnp.zeros(h1_ref.shape, F32)
    for kk in range(TOP_K):
        ffn = ffn + buf[kk] * gates[:, kk:kk + 1]
    o_ref[...] = _layer_norm(DEEPNORM_ALPHA * h1_ref[...] + ffn, g2_ref[...], b2_ref[...])


def _combine(dest_flat, h1, gates, ln2_g, ln2_b, y):
    t, d = h1.shape
    tm = 128
    return pl.pallas_call(
        _combine_kernel,
        grid_spec=pltpu.PrefetchScalarGridSpec(
            num_scalar_prefetch=1,
            grid=(t // tm,),
            in_specs=[pl.BlockSpec((tm, d), lambda i, dest: (i, 0)),
                      pl.BlockSpec((tm, LANES), lambda i, dest: (i, 0)),
                      pl.BlockSpec((1, d), lambda i, dest: (0, 0)),
                      pl.BlockSpec((1, d), lambda i, dest: (0, 0)),
                      pl.BlockSpec(memory_space=pl.ANY)],
            out_specs=pl.BlockSpec((tm, d), lambda i, dest: (i, 0)),
            scratch_shapes=[pltpu.VMEM((TOP_K, tm, d), F32), pltpu.SemaphoreType.DMA(())]),
        out_shape=jax.ShapeDtypeStruct((t, d), F32),
        compiler_params=_cparams(("arbitrary",)),
        name="moe_combine",
    )(dest_flat, h1, gates, ln2_g.reshape(1, d), ln2_b.reshape(1, d), y)


def kernel(x, ln_in_g, ln_in_b, w_in, b_gate, lam_re, lam_im, log_dt, b_re, b_im, c_re, c_im, d_skip,
           w_glu, w_attn_proj, w_o, ln1_g, ln1_b, w_router, b_router, w_up, b_up, w_down, b_down,
           ln2_g, ln2_b):
    n_batch, seq, d = x.shape
    assert d == D_MODEL and seq % ATT_TQ == 0 and w_in.shape[0] == DEPTH
    t = n_batch * seq
    xt = x.reshape(t, d)

    u, qt, k, vt, qit, ki, wt, gp = _inproj(xt, ln_in_g, ln_in_b, w_in[0], n_batch, seq)
    ys = _ssm(u, n_batch, seq, lam_re[0], lam_im[0], log_dt[0], b_re[0], b_im[0], c_re[0], c_im[0])
    att = _dsa(qt, k, vt, qit, ki, wt, n_batch, seq)
    h1, idx, gates = _merge(xt, ln_in_g, ln_in_b, ys, u, d_skip[0], att, gp, b_gate[0], w_glu[0],
                            w_attn_proj[0], w_o[0], ln1_g[0], ln1_b[0], w_router[0], b_router[0])

    rank, counts = _rank(idx)
    counts = counts[0, :N_EXPERTS].astype(I32)
    padded = (counts + MOE_ROWS - 1) // MOE_ROWS * MOE_ROWS
    pends = jnp.cumsum(padded)
    pstarts = pends - padded
    top_idx = idx[:, :TOP_K]
    dest = (pstarts[top_idx] + rank[:, :TOP_K]).reshape(-1)
    n_rows = t * TOP_K + N_EXPERTS * MOE_ROWS
    blk_start = jnp.arange(n_rows // MOE_ROWS, dtype=I32) * MOE_ROWS
    blk_expert = jnp.minimum(jnp.sum((pends[None, :] <= blk_start[:, None]).astype(I32), axis=1),
                             N_EXPERTS - 1)
    n_used = (pends[-1:] // MOE_ROWS).astype(I32)

    xs = _dispatch(dest, jnp.concatenate([jnp.zeros((1,), I32), pends]), h1, n_rows)
    y = _ffn(blk_expert, n_used, xs, w_up[0], b_up[0], w_down[0], b_down[0])
    out = _combine(dest, h1, gates, ln2_g[0], ln2_b[0], y)
    return out.reshape(n_batch, seq, d)
```

```python
import functools
import math

import jax
import jax.numpy as jnp
import numpy as np
from jax import lax
from jax.experimental import pallas as pl
from jax.experimental.pallas import tpu as pltpu

F32 = jnp.float32
BF16 = jnp.bfloat16
I32 = jnp.int32
I16 = jnp.int16

D_MODEL = 1024
SSM_WIDTH = 512
SSM_P = 16
SSM_G = 32
SSM_N = 64
SSM_CHUNK = 16
SSM_TG = 128 // SSM_P
N_HEADS = 8
HEAD_DIM = 64
ATTN_WIDTH = 512
IDX_HEADS = 8
IDX_DIM = 64
TOPK_KEYS = 256
N_EXPERTS = 32
TOP_K = 4
D_FF = 1024
SWIGLU_LIMIT = 7.0
SWIGLU_ALPHA = 1.702
LN_EPS = 1e-5
DEPTH = 1
DEEPNORM_ALPHA = (2 * DEPTH) ** 0.25

LANES = 128
SUBLANES = 8
ATT_TQ = 256
ATT_TK = 256
ATT_VROWS = HEAD_DIM + 16
ALIBI_SPLIT = 4
MOE_ROWS = 256
BISECT_STEPS = 32
NEG_BIG = -1e30
Q_SCALE = float(np.float32(HEAD_DIM ** -0.5 * math.log2(math.e)))
VMEM_LIMIT = 56 * 1024 * 1024


def _cparams(sem, vmem=VMEM_LIMIT):
    return pltpu.CompilerParams(dimension_semantics=sem, vmem_limit_bytes=vmem)


def _layer_norm(x, g, b):
    mu = jnp.mean(x, axis=-1, keepdims=True)
    xc = x - mu
    var = jnp.mean(xc * xc, axis=-1, keepdims=True)
    return xc * lax.rsqrt(var + LN_EPS) * g + b


def _inproj_kernel(x_ref, g_ref, b_ref, wu_ref, wqt_ref, wk_ref, wvt_ref, wqit_ref, wsm_ref, wsmt_ref, wg_ref,
                   ones_ref, u_ref, qt_ref, k_ref, vt_ref, qit_ref, ki_ref, wt_ref, gp_ref, *, seq):
    h = _layer_norm(x_ref[...], g_ref[...], b_ref[...])
    hb = h.astype(BF16)
    tm = x_ref.shape[0]

    def proj(w_ref):
        return jnp.dot(hb, w_ref[...], preferred_element_type=F32)

    def proj_t(w_ref):
        return lax.dot_general(w_ref[...], hb, (((1,), (1,)), ((), ())), preferred_element_type=F32)

    u = proj(wu_ref)
    for lt in range(SSM_WIDTH // LANES):
        u_ref[lt] = u[:, lt * LANES:(lt + 1) * LANES]
    qt_ref[...] = (proj_t(wqt_ref) * Q_SCALE).astype(BF16)
    shape = (tm, N_HEADS * LANES)
    pos = (pl.program_id(0) * tm + lax.broadcasted_iota(I32, shape, 0)) % seq
    sub = lax.broadcasted_iota(I32, shape, 1) % LANES - HEAD_DIM
    feat = jnp.where(jnp.logical_and(sub >= 0, sub < ALIBI_SPLIT), pos % ATT_TK,
                     jnp.where(jnp.logical_and(sub >= ALIBI_SPLIT, sub < 2 * ALIBI_SPLIT), pos // ATT_TK, 0))
    k_ref[...] = (proj(wk_ref) + feat.astype(F32)).astype(BF16)
    vt = (proj_t(wvt_ref) + ones_ref[...]).astype(BF16)
    for kb in range(tm // ATT_TK):
        vt_ref[kb] = vt[:, kb * ATT_TK:(kb + 1) * ATT_TK]
    qit_ref[...] = proj_t(wqit_ref).astype(BF16)
    ki_ref[...] = proj(wsm_ref)[:, :IDX_DIM].astype(BF16)
    wt_ref[...] = proj_t(wsmt_ref)[IDX_DIM:IDX_DIM + IDX_HEADS, :] * (IDX_DIM ** -0.5 * IDX_HEADS ** -0.5)
    gp_ref[...] = proj(wg_ref).astype(BF16)


def _inproj(xt, ln_g, ln_b, w_in, n_batch, seq):
    t, d = xt.shape
    tm = 512
    n_l = seq // tm
    o = 0
    ws = []
    for n in (SSM_WIDTH, ATTN_WIDTH, ATTN_WIDTH, ATTN_WIDTH, IDX_HEADS * IDX_DIM):
        ws.append(w_in[:, o:o + n].astype(BF16))
        o += n
    w_u, w_q, w_k, w_v, w_qi = ws
    w_k = jnp.pad(w_k.reshape(d, N_HEADS, HEAD_DIM),
                  ((0, 0), (0, 0), (0, LANES - HEAD_DIM))).reshape(d, N_HEADS * LANES)
    w_vt = jnp.pad(w_v.T.reshape(N_HEADS, HEAD_DIM, d),
                   ((0, 0), (0, ATT_VROWS - HEAD_DIM), (0, 0))).reshape(N_HEADS * ATT_VROWS, d)
    ones_col = jnp.zeros((N_HEADS, ATT_VROWS, 1), F32).at[:, HEAD_DIM, 0].set(1.0).reshape(-1, 1)
    n_small = IDX_DIM + IDX_HEADS
    w_small = jnp.pad(w_in[:, o:o + n_small], ((0, 0), (0, LANES - n_small))).astype(BF16)
    o += n_small
    w_gate = w_in[:, o:].astype(BF16)
    full = lambda a: pl.BlockSpec(a.shape, lambda i: (0,) * a.ndim)
    row = lambda n: pl.BlockSpec((tm, n), lambda i: (i, 0))
    col = lambda r: pl.BlockSpec((None, r, tm), lambda i: (i // n_l, 0, i % n_l))
    ins = [xt, ln_g.reshape(1, d), ln_b.reshape(1, d), w_u, w_q.T, w_k, w_vt, w_qi.T, w_small, w_small.T,
           w_gate, ones_col]
    nkb = seq // ATT_TK
    out_specs = [pl.BlockSpec((SSM_WIDTH // LANES, tm, LANES), lambda i: (0, i, 0)),
                 col(ATTN_WIDTH), row(N_HEADS * LANES),
                 pl.BlockSpec((None, tm // ATT_TK, N_HEADS * ATT_VROWS, ATT_TK),
                              lambda i: (i // n_l, i % n_l, 0, 0)),
                 col(IDX_HEADS * IDX_DIM), row(IDX_DIM), col(IDX_HEADS), row(2 * d)]
    out_shape = [jax.ShapeDtypeStruct((SSM_WIDTH // LANES, t, LANES), F32),
                 jax.ShapeDtypeStruct((n_batch, ATTN_WIDTH, seq), BF16),
                 jax.ShapeDtypeStruct((t, N_HEADS * LANES), BF16),
                 jax.ShapeDtypeStruct((n_batch, nkb, N_HEADS * ATT_VROWS, ATT_TK), BF16),
                 jax.ShapeDtypeStruct((n_batch, IDX_HEADS * IDX_DIM, seq), BF16),
                 jax.ShapeDtypeStruct((t, IDX_DIM), BF16),
                 jax.ShapeDtypeStruct((n_batch, IDX_HEADS, seq), F32),
                 jax.ShapeDtypeStruct((t, 2 * d), BF16)]
    return pl.pallas_call(
        functools.partial(_inproj_kernel, seq=seq),
        grid=(t // tm,),
        in_specs=[row(d)] + [full(a) for a in ins[1:]],
        out_specs=out_specs,
        out_shape=out_shape,
        compiler_params=_cparams(("parallel",)),
        name="inproj",
    )(*ins)


def _ssm_kernel(u_ref, lrr_ref, lir_ref, lrc_ref, lic_ref, ldt_ref, btr_ref, bti_ref, ctr_ref, cti_ref,
                tile_ref, spread_ref, y_ref, tg_sc, sin_sc, cout_sc, buf_a, buf_b, *, n_chunk, pad):
    cc, p, n, ng = SSM_CHUNK, SSM_P, SSM_N, SSM_TG
    hi = lax.Precision.HIGHEST
    width = cc * LANES

    def discretise(gl):
        dt = jnp.exp(ldt_ref[gl])
        return lrr_ref[gl] * dt, lir_ref[gl] * dt

    @pl.when(pl.program_id(1) == 0)
    def _():
        sin_sc[...] = jnp.zeros_like(sin_sc)
        w_rows = []
        for gl in range(ng):
            ar, ai = discretise(gl)
            mag = jnp.exp(ar)
            a_re, a_im = mag * jnp.cos(ai), mag * jnp.sin(ai)
            lr, li = lrr_ref[gl], lir_ref[gl]
            den = lr * lr + li * li
            f_re = ((a_re - 1.0) * lr + a_im * li) / den
            f_im = (a_im * lr - (a_re - 1.0) * li) / den
            bb_re = f_re * btr_ref[gl] - f_im * bti_ref[gl]
            bb_im = f_re * bti_ref[gl] + f_im * btr_ref[gl]
            dtc = jnp.exp(ldt_ref[gl])
            arc, aic = lrc_ref[gl] * dtc, lic_ref[gl] * dtc
            tau = (lax.broadcasted_iota(I32, (1, cc * p), 1) // p).astype(F32)
            rmag = jnp.exp(arc * tau)
            pw_re, pw_im = rmag * jnp.cos(aic * tau), rmag * jnp.sin(aic * tau)
            ct_re = jnp.dot(ctr_ref[gl], tile_ref[...], precision=hi, preferred_element_type=F32)
            ct_im = jnp.dot(cti_ref[gl], tile_ref[...], precision=hi, preferred_element_type=F32)
            r_re = pw_re * ct_re - pw_im * ct_im
            r_im = pw_re * ct_im + pw_im * ct_re
            m0 = (jnp.dot(bb_re, r_re, precision=hi, preferred_element_type=F32)
                  - jnp.dot(bb_im, r_im, precision=hi, preferred_element_type=F32))
            spread = jnp.dot(m0.astype(BF16), spread_ref[...], preferred_element_type=F32)
            w_rows.append(spread if gl == 0 else pltpu.roll(spread, gl * p, axis=1))
            e_pow = (cc - 1 - lax.broadcasted_iota(I32, (cc, 1), 0)).astype(F32)
            pmag = jnp.exp(ar * e_pow)
            ap_re, ap_im = pmag * jnp.cos(ai * e_pow), pmag * jnp.sin(ai * e_pow)
            for s in range(cc):
                pr, pi = ap_re[s:s + 1, :], ap_im[s:s + 1, :]
                sin_sc[s * LANES + gl * p:s * LANES + (gl + 1) * p, gl * LANES:(gl + 1) * LANES] = (
                    jnp.concatenate([bb_re * pr - bb_im * pi, bb_re * pi + bb_im * pr], axis=1).astype(BF16))
            mc = jnp.exp(arc)
            ac_re, ac_im = mc * jnp.cos(aic), mc * jnp.sin(aic)
            r1_re = ac_re * r_re - ac_im * r_im
            r1_im = ac_re * r_im + ac_im * r_re
            cout = jnp.concatenate([r1_re, -r1_im], axis=0).astype(BF16)
            cspread = jnp.dot(cout, spread_ref[...], preferred_element_type=F32)
            cout_sc[gl * LANES:(gl + 1) * LANES, :] = (
                cspread if gl == 0 else pltpu.roll(cspread, gl * p, axis=1)).astype(BF16)
        wb = jnp.concatenate(w_rows, axis=0).astype(BF16)
        for s in range(cc):
            tg_sc[s * LANES:(s + 1) * LANES, :] = wb if s == 0 else jnp.concatenate(
                [jnp.zeros((LANES, s * LANES), BF16), wb[:, :(cc - s) * LANES]], axis=1)

    @pl.when(jnp.logical_and(pl.program_id(0) == 0, pl.program_id(1) == 0))
    def _():
        buf_a[0:pad, :] = jnp.zeros((pad, ng * LANES), F32)
        buf_b[0:pad, :] = jnp.zeros((pad, ng * LANES), F32)

    u = u_ref[...].astype(BF16)
    y_intra = jnp.dot(u, tg_sc[...], preferred_element_type=F32)
    buf_a[pad:pad + n_chunk, :] = jnp.dot(u, sin_sc[...], preferred_element_type=F32)

    mults = []
    for gl in range(ng):
        ar, ai = discretise(gl)
        cm = jnp.exp(ar * cc)
        mults.append((cm * jnp.cos(ai * cc), cm * jnp.sin(ai * cc)))
    src, dst = buf_a, buf_b
    shift = 1
    while shift < n_chunk:
        for gl in range(ng):
            m_re, m_im = mults[gl]
            m_a = jnp.concatenate([m_re, m_re], axis=1)
            m_b = jnp.concatenate([-m_im, m_im], axis=1)
            ls = slice(gl * LANES, (gl + 1) * LANES)
            cur = src[pad:pad + n_chunk, ls]
            prev = src[pad - shift:pad - shift + n_chunk, ls]
            dst[pad:pad + n_chunk, ls] = cur + prev * m_a + pltpu.roll(prev, n, axis=1) * m_b
            mults[gl] = (m_re * m_re - m_im * m_im, 2.0 * m_re * m_im)
        src, dst = dst, src
        shift *= 2
    x_in = src[pad - 1:pad - 1 + n_chunk, :]
    y_ref[...] = y_intra + jnp.dot(x_in.astype(BF16), cout_sc[...], preferred_element_type=F32)


def _ssm(u_tiles, n_batch, seq, lam_re, lam_im, log_dt, b_re, b_im, c_re, c_im):
    g, p, n, cc, ng = SSM_G, SSM_P, SSM_N, SSM_CHUNK, SSM_TG
    n_tiles = g // ng
    n_chunk = seq // cc
    width = cc * LANES
    pad = max(n_chunk // 2, SUBLANES)
    u4 = u_tiles.reshape(n_tiles, n_batch, n_chunk, width)
    tau_p = np.arange(cc * p)
    tile_m = jnp.asarray((tau_p[None, :] % p == np.arange(p)[:, None]).astype(np.float32))
    spread_m = jnp.asarray((np.arange(width)[None, :] == ((tau_p // p) * LANES + tau_p % p)[:, None])
                           .astype(np.float32)).astype(BF16)
    params = [lam_re.reshape(g, 1, n), lam_im.reshape(g, 1, n), lam_re.reshape(g, n, 1),
              lam_im.reshape(g, n, 1), log_dt.reshape(g, 1, 1), b_re.transpose(0, 2, 1),
              b_im.transpose(0, 2, 1), c_re.transpose(0, 2, 1), c_im.transpose(0, 2, 1)]
    per_tile = lambda a: pl.BlockSpec((ng,) + a.shape[1:], lambda l, b: (l, 0, 0))
    const = lambda a: pl.BlockSpec(a.shape, lambda l, b: (0, 0))
    io_spec = pl.BlockSpec((None, None, n_chunk, width), lambda l, b: (l, b, 0, 0))
    y = pl.pallas_call(
        functools.partial(_ssm_kernel, n_chunk=n_chunk, pad=pad),
        grid=(n_tiles, n_batch),
        in_specs=[io_spec] + [per_tile(a) for a in params] + [const(tile_m), const(spread_m)],
        out_specs=io_spec,
        out_shape=jax.ShapeDtypeStruct(u4.shape, F32),
        scratch_shapes=[pltpu.VMEM((width, width), BF16), pltpu.VMEM((width, ng * LANES), BF16),
                        pltpu.VMEM((ng * LANES, width), BF16),
                        pltpu.VMEM((pad + n_chunk, ng * LANES), F32),
                        pltpu.VMEM((pad + n_chunk, ng * LANES), F32)],
        compiler_params=_cparams(("arbitrary", "arbitrary")),
        name="ssm",
    )(u4, *params, tile_m, spread_m)
    return y.reshape(n_tiles, n_batch * seq, LANES)


def _dsa_kernel(ki_ref, qit_ref, wt_ref, k_ref, qt_ref, vt_ref, slope_ref, o_ref,
                score_sc, qz_sc, acc_sc, m_sc, bias_sc, lg_sc, p_sc, *, n_sel):
    tq, tk = ATT_TQ, ATT_TK
    i = pl.program_id(1)
    nj = i + 1
    row = lax.broadcasted_iota(I32, (tk, tq), 0)
    col = lax.broadcasted_iota(I32, (tk, tq), 1)
    causal_diag = row <= col

    def fold8(x, op):
        return op(x.reshape(tk // SUBLANES, SUBLANES, tq), axis=0)

    def score_block(j, carry):
        lo8, hi8 = carry
        kib = ki_ref[j]
        s = jnp.zeros((tk, tq), F32)
        for h in range(IDX_HEADS):
            d = jnp.dot(kib, qit_ref[h * IDX_DIM:(h + 1) * IDX_DIM, :], preferred_element_type=F32)
            s = s + jnp.maximum(d, 0.0) * wt_ref[h:h + 1, :]
        causal = jnp.logical_or(j < i, causal_diag)
        score_sc[j] = jnp.where(causal, s, -jnp.inf)
        return (jnp.minimum(lo8, fold8(jnp.where(causal, s, jnp.inf), jnp.min)),
                jnp.maximum(hi8, fold8(jnp.where(causal, s, -jnp.inf), jnp.max)))

    lo8, hi8 = lax.fori_loop(0, nj, score_block, (jnp.full((SUBLANES, tq), jnp.inf, F32),
                                                  jnp.full((SUBLANES, tq), -jnp.inf, F32)))
    s_min = jnp.min(lo8, axis=0, keepdims=True)
    s_max = jnp.max(hi8, axis=0, keepdims=True)

    def count_ge(cand):
        def body(p, acc):
            j1 = jnp.minimum(2 * p + 1, nj - 1)
            w1 = jnp.where(2 * p + 1 < nj, 1.0, 0.0)
            return (acc + fold8(jnp.where(score_sc[2 * p] >= cand, 1.0, 0.0), jnp.sum)
                    + fold8(jnp.where(score_sc[j1] >= cand, w1, 0.0), jnp.sum))
        acc = lax.fori_loop(0, (nj + 1) // 2, body, jnp.zeros((SUBLANES, tq), F32))
        return acc.sum(axis=0, keepdims=True)

    def bisect_cond(state):
        it, n_open = state[0], state[1]
        return jnp.logical_and(it < BISECT_STEPS, n_open > 0)

    def bisect_step(state):
        it, _, open_q, lo, hi = state
        for _ in range(2):
            mid = lo + (hi - lo) * 0.5
            cnt = count_ge(mid)
            ok = cnt >= n_sel
            open_q = jnp.where(cnt == n_sel, 0, open_q)
            lo, hi = jnp.where(ok, mid, lo), jnp.where(ok, hi, mid)
        return it + 2, jnp.sum(open_q), open_q, lo, hi

    hi0 = s_max + jnp.abs(s_max) * 2.0 ** -20 + 1e-30
    state = (jnp.int32(0), jnp.int32(1), jnp.ones((1, tq), I32), s_min, hi0)
    thr_lo, thr_hi = lax.while_loop(bisect_cond, bisect_step, state)[3:]
    need_eq = n_sel - count_ge(thr_hi)

    for h in range(N_HEADS):
        qz_sc[h] = jnp.concatenate([qt_ref[h * HEAD_DIM:(h + 1) * HEAD_DIM, :], slope_ref[h]], axis=0)
    acc_sc[...] = jnp.zeros_like(acc_sc)
    m_sc[...] = jnp.full(m_sc.shape, NEG_BIG, F32)
    stril = (row > col).astype(BF16)
    n_chunk = 4
    rows_c = tk // n_chunk

    def attend_block(j, eq_seen):
        sb = score_sc[j]
        above = sb >= thr_hi
        eq = jnp.logical_and(sb >= thr_lo, jnp.logical_not(above))
        eq_f = jnp.where(eq, 1.0, 0.0).astype(BF16)
        before = jnp.dot(stril, eq_f, preferred_element_type=F32) + eq_seen
        sel = jnp.logical_or(above, jnp.logical_and(eq, before < need_eq))
        bias_sc[...] = jnp.where(sel, 0.0, -jnp.inf)
        block_max = []
        for h in range(N_HEADS):
            lg = jnp.dot(k_ref[j, :, h * LANES:(h + 1) * LANES], qz_sc[h],
                         preferred_element_type=F32) + bias_sc[...]
            lg_sc[h] = lg
            block_max.append(jnp.max(lg, axis=0, keepdims=True))
        m_old = m_sc[...]
        m_new = jnp.maximum(m_old, jnp.concatenate(block_max, axis=0))
        alpha = jnp.exp2(m_old - m_new)
        m_sc[...] = m_new
        for h in range(N_HEADS):
            for c in range(n_chunk):
                rs = slice(c * rows_c, (c + 1) * rows_c)
                p_sc[h, rs, :] = jnp.exp2(lg_sc[h, rs, :] - m_new[h:h + 1, :]).astype(BF16)
            pv = jnp.dot(vt_ref[j, h * ATT_VROWS:(h + 1) * ATT_VROWS, :], p_sc[h],
                         preferred_element_type=F32)
            sl = slice(h * ATT_VROWS, (h + 1) * ATT_VROWS)
            acc_sc[sl, :] = alpha[h:h + 1, :] * acc_sc[sl, :] + pv
        return eq_seen + jnp.sum(eq_f.astype(F32), axis=0, keepdims=True)

    lax.fori_loop(0, nj, attend_block, jnp.zeros((1, tq), F32))
    for h in range(N_HEADS):
        base = h * ATT_VROWS
        o_ref[h * HEAD_DIM:(h + 1) * HEAD_DIM, :] = (
            acc_sc[base:base + HEAD_DIM, :] / acc_sc[base + HEAD_DIM:base + HEAD_DIM + 1, :]).astype(o_ref.dtype)


def _alibi_rows(tq):
    rows = []
    for h in range(N_HEADS):
        rest = 2.0 ** (-8.0 * (h + 1) / N_HEADS) * Q_SCALE * HEAD_DIM ** 0.5
        pieces = []
        for _ in range(ALIBI_SPLIT):
            piece = float(np.asarray(rest, dtype=BF16).astype(np.float32))
            pieces.append(piece)
            rest -= piece
        col = pieces + [ATT_TK * x for x in pieces] + [0.0] * (HEAD_DIM - 2 * ALIBI_SPLIT)
        rows.append(col)
    return jnp.broadcast_to(jnp.asarray(np.asarray(rows, np.float32))[:, :, None],
                            (N_HEADS, HEAD_DIM, tq)).astype(BF16)


def _dsa(qt, k, vt, qit, ki, wt, n_batch, seq):
    tq, tk = ATT_TQ, ATT_TK
    nkb = seq // tk
    n_sel = min(TOPK_KEYS, seq // 4)
    kb = k.reshape(n_batch, nkb, tk, N_HEADS * LANES)
    kib = ki.reshape(n_batch, nkb, tk, IDX_DIM)
    per_batch = lambda shape: pl.BlockSpec((None,) + shape, lambda b, i: (b, 0, 0, 0),
                                           pipeline_mode=pl.Buffered(1))
    per_q = lambda r: pl.BlockSpec((None, r, tq), lambda b, i: (b, 0, i))
    return pl.pallas_call(
        functools.partial(_dsa_kernel, n_sel=n_sel),
        grid=(n_batch, seq // tq),
        in_specs=[per_batch((nkb, tk, IDX_DIM)), per_q(IDX_HEADS * IDX_DIM), per_q(IDX_HEADS),
                  per_batch((nkb, tk, N_HEADS * LANES)), per_q(ATTN_WIDTH),
                  per_batch((nkb, N_HEADS * ATT_VROWS, tk)),
                  pl.BlockSpec((N_HEADS, HEAD_DIM, tq), lambda b, i: (0, 0, 0))],
        out_specs=per_q(ATTN_WIDTH),
        out_shape=jax.ShapeDtypeStruct((n_batch, ATTN_WIDTH, seq), BF16),
        scratch_shapes=[pltpu.VMEM((nkb, tk, tq), F32), pltpu.VMEM((N_HEADS, LANES, tq), BF16),
                        pltpu.VMEM((N_HEADS * ATT_VROWS, tq), F32), pltpu.VMEM((N_HEADS, tq), F32),
                        pltpu.VMEM((tk, tq), F32), pltpu.VMEM((N_HEADS, tk, tq), F32),
                        pltpu.VMEM((N_HEADS, tk, tq), BF16)],
        compiler_params=_cparams(("parallel", "arbitrary")),
        name="dsa",
    )(kib, qit, wt, kb, qt, vt, _alibi_rows(tq))


def _merge_kernel(x_ref, g0_ref, b0_ref, ys_ref, u_ref, dsk_ref, att_ref, gp_ref, bg_ref,
                  wglu_ref, wap_ref, wo_ref, g1_ref, b1_ref, wr_ref, br_ref,
                  h1_ref, idx_ref, gate_ref):
    d = D_MODEL
    h = _layer_norm(x_ref[...], g0_ref[...], b0_ref[...])
    tiles = range(SSM_WIDTH // LANES)
    ys = jnp.concatenate([ys_ref[lt] for lt in tiles], axis=1)
    u = jnp.concatenate([u_ref[lt] for lt in tiles], axis=1)
    y = jax.nn.gelu(ys + dsk_ref[...] * u)
    glu = jnp.dot(y.astype(BF16), wglu_ref[...], preferred_element_type=F32)
    y_ssm = glu[:, :d] * jax.nn.sigmoid(glu[:, d:])
    y_att = lax.dot_general(att_ref[...], wap_ref[...], (((0,), (0,)), ((), ())),
                            preferred_element_type=F32)
    gs = jax.nn.sigmoid(gp_ref[...].astype(F32) + bg_ref[...])
    mixed = gs[:, :d] * y_ssm + gs[:, d:] * y_att
    mix = jnp.dot(mixed.astype(BF16), wo_ref[...], preferred_element_type=F32)
    h1 = _layer_norm(DEEPNORM_ALPHA * h + mix, g1_ref[...], b1_ref[...])
    h1_ref[...] = h1
    logits = jnp.dot(h1, wr_ref[...], precision=lax.Precision.HIGHEST,
                     preferred_element_type=F32) + br_ref[...]
    lane = lax.broadcasted_iota(I32, logits.shape, 1)
    logits = jnp.where(lane < N_EXPERTS, logits, -jnp.inf)
    idx_out = jnp.zeros(logits.shape, I32)
    val_out = jnp.zeros(logits.shape, F32)
    vals = []
    for kk in range(TOP_K):
        vmax = jnp.max(logits, axis=-1, keepdims=True)
        imax = jnp.min(jnp.where(logits == vmax, lane, LANES), axis=-1, keepdims=True)
        idx_out = jnp.where(lane == kk, imax, idx_out)
        vals.append(vmax)
        logits = jnp.where(lane == imax, -jnp.inf, logits)
    es = [jnp.exp(vk - vals[0]) for vk in vals]
    den = es[0] + es[1] + es[2] + es[3]
    for kk in range(TOP_K):
        val_out = jnp.where(lane == kk, es[kk] / den, val_out)
    idx_ref[...] = idx_out
    gate_ref[...] = val_out


def _merge(xt, ln0_g, ln0_b, ys, u, d_skip, att, gp, b_gate, w_glu, w_ap, w_o, ln1_g, ln1_b,
           w_router, b_router):
    t, d = xt.shape
    tm = 256
    wr = jnp.pad(w_router, ((0, 0), (0, LANES - N_EXPERTS)))
    br = jnp.pad(b_router, (0, LANES - N_EXPERTS)).reshape(1, LANES)
    ins = [xt, ln0_g.reshape(1, d), ln0_b.reshape(1, d), ys, u, d_skip.reshape(1, SSM_WIDTH), att, gp,
           b_gate.reshape(1, 2 * d), w_glu.astype(BF16), w_ap.astype(BF16), w_o.astype(BF16),
           ln1_g.reshape(1, d), ln1_b.reshape(1, d), wr, br]
    tiled = {0, 7}
    n_l = att.shape[2] // tm
    in_specs = []
    for n, a in enumerate(ins):
        if n in tiled:
            in_specs.append(pl.BlockSpec((tm, a.shape[1]), lambda i: (i, 0)))
        elif n in (3, 4):
            in_specs.append(pl.BlockSpec((a.shape[0], tm, LANES), lambda i: (0, i, 0)))
        elif n == 6:
            in_specs.append(pl.BlockSpec((None, ATTN_WIDTH, tm), lambda i: (i // n_l, 0, i % n_l)))
        else:
            in_specs.append(pl.BlockSpec(a.shape, lambda i: (0, 0)))
    return pl.pallas_call(
        _merge_kernel,
        grid=(t // tm,),
        in_specs=in_specs,
        out_specs=[pl.BlockSpec((tm, d), lambda i: (i, 0)), pl.BlockSpec((tm, LANES), lambda i: (i, 0)),
                   pl.BlockSpec((tm, LANES), lambda i: (i, 0))],
        out_shape=[jax.ShapeDtypeStruct((t, d), F32), jax.ShapeDtypeStruct((t, LANES), I32),
                   jax.ShapeDtypeStruct((t, LANES), F32)],
        compiler_params=_cparams(("parallel",)),
        name="merge",
    )(*ins)


def _rank_kernel(idx_ref, rank_ref, cnt_ref, run_sc):
    tm = idx_ref.shape[0]

    @pl.when(pl.program_id(0) == 0)
    def _():
        run_sc[...] = jnp.zeros_like(run_sc)

    idx = idx_ref[...]
    lane = lax.broadcasted_iota(I32, (tm, LANES), 1)
    hots = [lane == idx[:, kk:kk + 1] for kk in range(TOP_K)]
    multi = sum(hh.astype(F32) for hh in hots)
    r_i = lax.broadcasted_iota(I32, (tm, tm), 0)
    c_i = lax.broadcasted_iota(I32, (tm, tm), 1)
    before = jnp.dot((c_i < r_i).astype(BF16), multi.astype(BF16), preferred_element_type=F32)
    base = before + run_sc[...]
    out = jnp.zeros((tm, LANES), I32)
    for kk in range(TOP_K):
        rk = jnp.sum(jnp.where(hots[kk], base, 0.0), axis=-1, keepdims=True)
        out = jnp.where(lane == kk, rk.astype(I32), out)
    rank_ref[...] = out
    run_sc[...] = run_sc[...] + jnp.sum(multi, axis=0, keepdims=True)
    cnt_ref[...] = run_sc[...]


def _rank(idx):
    t = idx.shape[0]
    tm = 512
    return pl.pallas_call(
        _rank_kernel,
        grid=(t // tm,),
        in_specs=[pl.BlockSpec((tm, LANES), lambda i: (i, 0))],
        out_specs=[pl.BlockSpec((tm, LANES), lambda i: (i, 0)), pl.BlockSpec((1, LANES), lambda i: (0, 0))],
        out_shape=[jax.ShapeDtypeStruct((t, LANES), I32), jax.ShapeDtypeStruct((1, LANES), F32)],
        scratch_shapes=[pltpu.VMEM((1, LANES), F32)],
        compiler_params=_cparams(("arbitrary",)),
        name="moe_rank",
    )(idx)


def _row_copy(src_ref, src_row, dst_ref, dst_row, sem):
    return pltpu.make_async_copy(src_ref.at[pl.ds(src_row, 1), :], dst_ref.at[pl.ds(dst_row, 1), :], sem)


def _dispatch_kernel(dest_ref, pend_ref, h_ref, xs_ref, zero_sc, sem, zsem):
    tm = h_ref.shape[0]
    base = pl.program_id(0) * tm * TOP_K

    @pl.when(pl.program_id(0) == 0)
    def _():
        zero_sc[...] = jnp.zeros_like(zero_sc)

        def zero_copy(e):
            start = pl.multiple_of(pend_ref[e + 1] - MOE_ROWS, MOE_ROWS)
            return pltpu.make_async_copy(zero_sc, xs_ref.at[pl.ds(start, MOE_ROWS), :], zsem)

        for e in range(N_EXPERTS):
            @pl.when(pend_ref[e + 1] > pend_ref[e])
            def _():
                zero_copy(e).start()
        for e in range(N_EXPERTS):
            @pl.when(pend_ref[e + 1] > pend_ref[e])
            def _():
                zero_copy(e).wait()

        def tail_copy(blk):
            return pltpu.make_async_copy(
                zero_sc, xs_ref.at[pl.ds(pl.multiple_of(blk * MOE_ROWS, MOE_ROWS), MOE_ROWS), :], zsem)

        first_free = pend_ref[N_EXPERTS] // MOE_ROWS
        n_blocks = xs_ref.shape[0] // MOE_ROWS
        lax.fori_loop(first_free, n_blocks, lambda blk, c: (tail_copy(blk).start(), c)[1], 0)
        lax.fori_loop(first_free, n_blocks, lambda blk, c: (tail_copy(blk).wait(), c)[1], 0)

    def start(r, c):
        for kk in range(TOP_K):
            _row_copy(h_ref, r, xs_ref, dest_ref[base + r * TOP_K + kk], sem).start()
        return c

    lax.fori_loop(0, tm, start, 0)

    def wait(r, c):
        for kk in range(TOP_K):
            _row_copy(h_ref, 0, xs_ref, 0, sem).wait()
        return c

    lax.fori_loop(0, tm, wait, 0)


def _dispatch(dest_flat, pend0, h1, n_rows):
    t, d = h1.shape
    tm = 128
    return pl.pallas_call(
        _dispatch_kernel,
        grid_spec=pltpu.PrefetchScalarGridSpec(
            num_scalar_prefetch=2,
            grid=(t // tm,),
            in_specs=[pl.BlockSpec((tm, d), lambda i, dest, pend: (i, 0))],
            out_specs=pl.BlockSpec(memory_space=pl.ANY),
            scratch_shapes=[pltpu.VMEM((MOE_ROWS, d), F32), pltpu.SemaphoreType.DMA(()),
                            pltpu.SemaphoreType.DMA(())]),
        out_shape=jax.ShapeDtypeStruct((n_rows, d), F32),
        compiler_params=_cparams(("arbitrary",)),
        name="moe_dispatch",
    )(dest_flat, pend0, h1)


def _ffn_kernel(be_ref, nused_ref, xs_ref, wup_ref, bup_ref, wdn_ref, bdn_ref, y_ref, wup_sc, wdn_sc):
    b = pl.program_id(0)
    used = b < nused_ref[0]

    @pl.when(jnp.logical_and(used, jnp.logical_or(b == 0, be_ref[b] != be_ref[jnp.maximum(b - 1, 0)])))
    def _():
        wup_sc[...] = wup_ref[...].astype(BF16)
        wdn_sc[...] = wdn_ref[...].astype(BF16)

    @pl.when(used)
    def _():
        x = xs_ref[...].astype(BF16)
        hdn = jnp.dot(x, wup_sc[...], preferred_element_type=F32) + bup_ref[...]
        h_gate = jnp.minimum(hdn[:, :D_FF], SWIGLU_LIMIT)
        h_lin = jnp.clip(hdn[:, D_FF:], -SWIGLU_LIMIT, SWIGLU_LIMIT)
        act = (h_lin + 1.0) * (h_gate * jax.nn.sigmoid(SWIGLU_ALPHA * h_gate))
        y_ref[...] = jnp.dot(act.astype(BF16), wdn_sc[...], preferred_element_type=F32) + bdn_ref[...]

    @pl.when(b >= nused_ref[0])
    def _():
        y_ref[...] = jnp.zeros_like(y_ref)


def _ffn(blk_expert, n_used, xs, w_up, b_up, w_down, b_down):
    n_rows, d = xs.shape
    r = MOE_ROWS
    e = N_EXPERTS
    return pl.pallas_call(
        _ffn_kernel,
        grid_spec=pltpu.PrefetchScalarGridSpec(
            num_scalar_prefetch=2,
            grid=(n_rows // r,),
            in_specs=[pl.BlockSpec((r, d), lambda b, be, nu: (jnp.minimum(b, nu[0] - 1), 0)),
                      pl.BlockSpec((None, d, 2 * D_FF), lambda b, be, nu: (be[b], 0, 0)),
                      pl.BlockSpec((None, 1, 2 * D_FF), lambda b, be, nu: (be[b], 0, 0)),
                      pl.BlockSpec((None, D_FF, d), lambda b, be, nu: (be[b], 0, 0)),
                      pl.BlockSpec((None, 1, d), lambda b, be, nu: (be[b], 0, 0))],
            out_specs=pl.BlockSpec((r, d), lambda b, be, nu: (b, 0)),
            scratch_shapes=[pltpu.VMEM((d, 2 * D_FF), BF16), pltpu.VMEM((D_FF, d), BF16)]),
        out_shape=jax.ShapeDtypeStruct((n_rows, d), F32),
        compiler_params=_cparams(("arbitrary",)),
        name="moe_ffn",
    )(blk_expert, n_used, xs, w_up, b_up.reshape(e, 1, 2 * D_FF), w_down, b_down.reshape(e, 1, d))


def _combine_kernel(dest_ref, h1_ref, gate_ref, g2_ref, b2_ref, y_ref, o_ref, buf, sem):
    tm = h1_ref.shape[0]
    step = pl.program_id(0)
    slot = step % 2

    def gather(tile):
        sl = tile % 2

        def start(r, c):
            for kk in range(TOP_K):
                _row_copy(y_ref, dest_ref[(tile * tm + r) * TOP_K + kk], buf.at[sl, kk], r, sem.at[sl]).start()
            return c

        lax.fori_loop(0, tm, start, 0)

    @pl.when(step == 0)
    def _():
        gather(0)

    @pl.when(step + 1 < pl.num_programs(0))
    def _():
        gather(step + 1)

    def wait(r, c):
        for kk in range(TOP_K):
            _row_copy(y_ref, 0, buf.at[slot, kk], 0, sem.at[slot]).wait()
        return c

    lax.fori_loop(0, tm, wait, 0)
    gates = gate_ref[...]
    ffn = jnp.zeros(h1_ref.shape, F32)
    for kk in range(TOP_K):
        ffn = ffn + buf[slot, kk] * gates[:, kk:kk + 1]
    o_ref[...] = _layer_norm(DEEPNORM_ALPHA * h1_ref[...] + ffn, g2_ref[...], b2_ref[...])


def _combine(dest_flat, h1, gates, ln2_g, ln2_b, y):
    t, d = h1.shape
    tm = 128
    return pl.pallas_call(
        _combine_kernel,
        grid_spec=pltpu.PrefetchScalarGridSpec(
            num_scalar_prefetch=1,
            grid=(t // tm,),
            in_specs=[pl.BlockSpec((tm, d), lambda i, dest: (i, 0)),
                      pl.BlockSpec((tm, LANES), lambda i, dest: (i, 0)),
                      pl.BlockSpec((1, d), lambda i, dest: (0, 0)),
                      pl.BlockSpec((1, d), lambda i, dest: (0, 0)),
                      pl.BlockSpec(memory_space=pl.ANY)],
            out_specs=pl.BlockSpec((tm, d), lambda i, dest: (i, 0)),
            scratch_shapes=[pltpu.VMEM((2, TOP_K, tm, d), F32), pltpu.SemaphoreType.DMA((2,))]),
        out_shape=jax.ShapeDtypeStruct((t, d), F32),
        compiler_params=_cparams(("arbitrary",)),
        name="moe_combine",
    )(dest_flat, h1, gates, ln2_g.reshape(1, d), ln2_b.reshape(1, d), y)


def kernel(x, ln_in_g, ln_in_b, w_in, b_gate, lam_re, lam_im, log_dt, b_re, b_im, c_re, c_im, d_skip,
           w_glu, w_attn_proj, w_o, ln1_g, ln1_b, w_router, b_router, w_up, b_up, w_down, b_down,
           ln2_g, ln2_b):
    n_batch, seq, d = x.shape
    assert d == D_MODEL and seq % ATT_TQ == 0 and w_in.shape[0] == DEPTH
    t = n_batch * seq
    xt = x.reshape(t, d)

    u, qt, k, vt, qit, ki, wt, gp = _inproj(xt, ln_in_g, ln_in_b, w_in[0], n_batch, seq)
    ys = _ssm(u, n_batch, seq, lam_re[0], lam_im[0], log_dt[0], b_re[0], b_im[0], c_re[0], c_im[0])
    att = _dsa(qt, k, vt, qit, ki, wt, n_batch, seq)
    h1, idx, gates = _merge(xt, ln_in_g, ln_in_b, ys, u, d_skip[0], att, gp, b_gate[0], w_glu[0],
                            w_attn_proj[0], w_o[0], ln1_g[0], ln1_b[0], w_router[0], b_router[0])

    rank, counts = _rank(idx)
    counts = counts[0, :N_EXPERTS].astype(I32)
    padded = (counts + MOE_ROWS - 1) // MOE_ROWS * MOE_ROWS
    pends = jnp.cumsum(padded)
    pstarts = pends - padded
    top_idx = idx[:, :TOP_K]
    dest = (pstarts[top_idx] + rank[:, :TOP_K]).reshape(-1)
    n_rows = t * TOP_K + N_EXPERTS * MOE_ROWS
    blk_start = jnp.arange(n_rows // MOE_ROWS, dtype=I32) * MOE_ROWS
    blk_expert = jnp.minimum(jnp.sum((pends[None, :] <= blk_start[:, None]).astype(I32), axis=1),
                             N_EXPERTS - 1)
    n_used = (pends[-1:] // MOE_ROWS).astype(I32)

    xs = _dispatch(dest, jnp.concatenate([jnp.zeros((1,), I32), pends]), h1, n_rows)
    y = _ffn(blk_expert, n_used, xs, w_up[0], b_up[0], w_down[0], b_down[0])
    out = _combine(dest, h1, gates, ln2_g[0], ln2_b[0], y)
    return out.reshape(n_batch, seq, d)
```

```python
import functools
import math

import jax
import jax.numpy as jnp
import numpy as np
from jax import lax
from jax.experimental import pallas as pl
from jax.experimental.pallas import tpu as pltpu

F32 = jnp.float32
BF16 = jnp.bfloat16
I32 = jnp.int32
I16 = jnp.int16

D_MODEL = 1024
SSM_WIDTH = 512
SSM_P = 16
SSM_G = 32
SSM_N = 64
SSM_CHUNK = 16
SSM_TG = 128 // SSM_P
N_HEADS = 8
HEAD_DIM = 64
ATTN_WIDTH = 512
IDX_HEADS = 8
IDX_DIM = 64
TOPK_KEYS = 256
N_EXPERTS = 32
TOP_K = 4
D_FF = 1024
SWIGLU_LIMIT = 7.0
SWIGLU_ALPHA = 1.702
LN_EPS = 1e-5
DEPTH = 1
DEEPNORM_ALPHA = (2 * DEPTH) ** 0.25

LANES = 128
SUBLANES = 8
ATT_TQ = 256
ATT_TK = 256
ATT_VROWS = HEAD_DIM + 16
ALIBI_SPLIT = 4
MOE_ROWS = 256
BISECT_STEPS = 32
NEG_BIG = -1e30
Q_SCALE = float(np.float32(HEAD_DIM ** -0.5 * math.log2(math.e)))
VMEM_LIMIT = 56 * 1024 * 1024


def _cparams(sem, vmem=VMEM_LIMIT):
    return pltpu.CompilerParams(dimension_semantics=sem, vmem_limit_bytes=vmem)


def _layer_norm(x, g, b):
    mu = jnp.mean(x, axis=-1, keepdims=True)
    xc = x - mu
    var = jnp.mean(xc * xc, axis=-1, keepdims=True)
    return xc * lax.rsqrt(var + LN_EPS) * g + b


def _inproj_kernel(x_ref, g_ref, b_ref, wu_ref, wqt_ref, wk_ref, wvt_ref, wqit_ref, wsm_ref, wsmt_ref, wg_ref,
                   ones_ref, u_ref, qt_ref, k_ref, vt_ref, qit_ref, ki_ref, wt_ref, gp_ref, *, seq):
    h = _layer_norm(x_ref[...], g_ref[...], b_ref[...])
    hb = h.astype(BF16)
    tm = x_ref.shape[0]

    def proj(w_ref):
        return jnp.dot(hb, w_ref[...], preferred_element_type=F32)

    def proj_t(w_ref):
        return lax.dot_general(w_ref[...], hb, (((1,), (1,)), ((), ())), preferred_element_type=F32)

    u = proj(wu_ref)
    for lt in range(SSM_WIDTH // LANES):
        u_ref[lt] = u[:, lt * LANES:(lt + 1) * LANES]
    qt_ref[...] = (proj_t(wqt_ref) * Q_SCALE).astype(BF16)
    shape = (tm, N_HEADS * LANES)
    pos = (pl.program_id(0) * tm + lax.broadcasted_iota(I32, shape, 0)) % seq
    sub = lax.broadcasted_iota(I32, shape, 1) % LANES - HEAD_DIM
    feat = jnp.where(jnp.logical_and(sub >= 0, sub < ALIBI_SPLIT), pos % ATT_TK,
                     jnp.where(jnp.logical_and(sub >= ALIBI_SPLIT, sub < 2 * ALIBI_SPLIT), pos // ATT_TK, 0))
    k_ref[...] = (proj(wk_ref) + feat.astype(F32)).astype(BF16)
    vt = (proj_t(wvt_ref) + ones_ref[...]).astype(BF16)
    for kb in range(tm // ATT_TK):
        vt_ref[kb] = vt[:, kb * ATT_TK:(kb + 1) * ATT_TK]
    qit_ref[...] = proj_t(wqit_ref).astype(BF16)
    ki_ref[...] = proj(wsm_ref)[:, :IDX_DIM].astype(BF16)
    wt_ref[...] = proj_t(wsmt_ref)[IDX_DIM:IDX_DIM + IDX_HEADS, :] * (IDX_DIM ** -0.5 * IDX_HEADS ** -0.5)
    gp_ref[...] = proj(wg_ref).astype(BF16)


def _inproj(xt, ln_g, ln_b, w_in, n_batch, seq):
    t, d = xt.shape
    tm = 512
    n_l = seq // tm
    o = 0
    ws = []
    for n in (SSM_WIDTH, ATTN_WIDTH, ATTN_WIDTH, ATTN_WIDTH, IDX_HEADS * IDX_DIM):
        ws.append(w_in[:, o:o + n].astype(BF16))
        o += n
    w_u, w_q, w_k, w_v, w_qi = ws
    w_k = jnp.pad(w_k.reshape(d, N_HEADS, HEAD_DIM),
                  ((0, 0), (0, 0), (0, LANES - HEAD_DIM))).reshape(d, N_HEADS * LANES)
    w_vt = jnp.pad(w_v.T.reshape(N_HEADS, HEAD_DIM, d),
                   ((0, 0), (0, ATT_VROWS - HEAD_DIM), (0, 0))).reshape(N_HEADS * ATT_VROWS, d)
    ones_col = jnp.zeros((N_HEADS, ATT_VROWS, 1), F32).at[:, HEAD_DIM, 0].set(1.0).reshape(-1, 1)
    n_small = IDX_DIM + IDX_HEADS
    w_small = jnp.pad(w_in[:, o:o + n_small], ((0, 0), (0, LANES - n_small))).astype(BF16)
    o += n_small
    w_gate = w_in[:, o:].astype(BF16)
    full = lambda a: pl.BlockSpec(a.shape, lambda i: (0,) * a.ndim)
    row = lambda n: pl.BlockSpec((tm, n), lambda i: (i, 0))
    col = lambda r: pl.BlockSpec((None, r, tm), lambda i: (i // n_l, 0, i % n_l))
    ins = [xt, ln_g.reshape(1, d), ln_b.reshape(1, d), w_u, w_q.T, w_k, w_vt, w_qi.T, w_small, w_small.T,
           w_gate, ones_col]
    nkb = seq // ATT_TK
    out_specs = [pl.BlockSpec((SSM_WIDTH // LANES, tm, LANES), lambda i: (0, i, 0)),
                 col(ATTN_WIDTH), row(N_HEADS * LANES),
                 pl.BlockSpec((None, tm // ATT_TK, N_HEADS * ATT_VROWS, ATT_TK),
                              lambda i: (i // n_l, i % n_l, 0, 0)),
                 col(IDX_HEADS * IDX_DIM), row(IDX_DIM), col(IDX_HEADS), row(2 * d)]
    out_shape = [jax.ShapeDtypeStruct((SSM_WIDTH // LANES, t, LANES), F32),
                 jax.ShapeDtypeStruct((n_batch, ATTN_WIDTH, seq), BF16),
                 jax.ShapeDtypeStruct((t, N_HEADS * LANES), BF16),
                 jax.ShapeDtypeStruct((n_batch, nkb, N_HEADS * ATT_VROWS, ATT_TK), BF16),
                 jax.ShapeDtypeStruct((n_batch, IDX_HEADS * IDX_DIM, seq), BF16),
                 jax.ShapeDtypeStruct((t, IDX_DIM), BF16),
                 jax.ShapeDtypeStruct((n_batch, IDX_HEADS, seq), F32),
                 jax.ShapeDtypeStruct((t, 2 * d), BF16)]
    return pl.pallas_call(
        functools.partial(_inproj_kernel, seq=seq),
        grid=(t // tm,),
        in_specs=[row(d)] + [full(a) for a in ins[1:]],
        out_specs=out_specs,
        out_shape=out_shape,
        compiler_params=_cparams(("parallel",)),
        name="inproj",
    )(*ins)


def _ssm_kernel(u_ref, lrr_ref, lir_ref, lrc_ref, lic_ref, ldt_ref, btr_ref, bti_ref, ctr_ref, cti_ref,
                tile_ref, spread_ref, y_ref, tg_sc, sin_sc, cout_sc, buf_a, buf_b, *, n_chunk, pad):
    cc, p, n, ng = SSM_CHUNK, SSM_P, SSM_N, SSM_TG
    hi = lax.Precision.HIGHEST
    width = cc * LANES

    def discretise(gl):
        dt = jnp.exp(ldt_ref[gl])
        return lrr_ref[gl] * dt, lir_ref[gl] * dt

    @pl.when(pl.program_id(1) == 0)
    def _():
        sin_sc[...] = jnp.zeros_like(sin_sc)
        w_rows = []
        for gl in range(ng):
            ar, ai = discretise(gl)
            mag = jnp.exp(ar)
            a_re, a_im = mag * jnp.cos(ai), mag * jnp.sin(ai)
            lr, li = lrr_ref[gl], lir_ref[gl]
            den = lr * lr + li * li
            f_re = ((a_re - 1.0) * lr + a_im * li) / den
            f_im = (a_im * lr - (a_re - 1.0) * li) / den
            bb_re = f_re * btr_ref[gl] - f_im * bti_ref[gl]
            bb_im = f_re * bti_ref[gl] + f_im * btr_ref[gl]
            dtc = jnp.exp(ldt_ref[gl])
            arc, aic = lrc_ref[gl] * dtc, lic_ref[gl] * dtc
            tau = (lax.broadcasted_iota(I32, (1, cc * p), 1) // p).astype(F32)
            rmag = jnp.exp(arc * tau)
            pw_re, pw_im = rmag * jnp.cos(aic * tau), rmag * jnp.sin(aic * tau)
            ct_re = jnp.dot(ctr_ref[gl], tile_ref[...], precision=hi, preferred_element_type=F32)
            ct_im = jnp.dot(cti_ref[gl], tile_ref[...], precision=hi, preferred_element_type=F32)
            r_re = pw_re * ct_re - pw_im * ct_im
            r_im = pw_re * ct_im + pw_im * ct_re
            m0 = (jnp.dot(bb_re, r_re, precision=hi, preferred_element_type=F32)
                  - jnp.dot(bb_im, r_im, precision=hi, preferred_element_type=F32))
            spread = jnp.dot(m0.astype(BF16), spread_ref[...], preferred_element_type=F32)
            w_rows.append(spread if gl == 0 else pltpu.roll(spread, gl * p, axis=1))
            e_pow = (cc - 1 - lax.broadcasted_iota(I32, (cc, 1), 0)).astype(F32)
            pmag = jnp.exp(ar * e_pow)
            ap_re, ap_im = pmag * jnp.cos(ai * e_pow), pmag * jnp.sin(ai * e_pow)
            for s in range(cc):
                pr, pi = ap_re[s:s + 1, :], ap_im[s:s + 1, :]
                sin_sc[s * LANES + gl * p:s * LANES + (gl + 1) * p, gl * LANES:(gl + 1) * LANES] = (
                    jnp.concatenate([bb_re * pr - bb_im * pi, bb_re * pi + bb_im * pr], axis=1).astype(BF16))
            mc = jnp.exp(arc)
            ac_re, ac_im = mc * jnp.cos(aic), mc * jnp.sin(aic)
            r1_re = ac_re * r_re - ac_im * r_im
            r1_im = ac_re * r_im + ac_im * r_re
            cout = jnp.concatenate([r1_re, -r1_im], axis=0).astype(BF16)
            cspread = jnp.dot(cout, spread_ref[...], preferred_element_type=F32)
            cout_sc[gl * LANES:(gl + 1) * LANES, :] = (
                cspread if gl == 0 else pltpu.roll(cspread, gl * p, axis=1)).astype(BF16)
        wb = jnp.concatenate(w_rows, axis=0).astype(BF16)
        for s in range(cc):
            tg_sc[s * LANES:(s + 1) * LANES, :] = wb if s == 0 else jnp.concatenate(
                [jnp.zeros((LANES, s * LANES), BF16), wb[:, :(cc - s) * LANES]], axis=1)

    @pl.when(jnp.logical_and(pl.program_id(0) == 0, pl.program_id(1) == 0))
    def _():
        buf_a[0:pad, :] = jnp.zeros((pad, ng * LANES), F32)
        buf_b[0:pad, :] = jnp.zeros((pad, ng * LANES), F32)

    u = u_ref[...].astype(BF16)
    y_intra = jnp.dot(u, tg_sc[...], preferred_element_type=F32)
    buf_a[pad:pad + n_chunk, :] = jnp.dot(u, sin_sc[...], preferred_element_type=F32)

    mults = []
    for gl in range(ng):
        ar, ai = discretise(gl)
        cm = jnp.exp(ar * cc)
        mults.append((cm * jnp.cos(ai * cc), cm * jnp.sin(ai * cc)))
    src, dst = buf_a, buf_b
    shift = 1
    while shift < n_chunk:
        for gl in range(ng):
            m_re, m_im = mults[gl]
            m_a = jnp.concatenate([m_re, m_re], axis=1)
            m_b = jnp.concatenate([-m_im, m_im], axis=1)
            ls = slice(gl * LANES, (gl + 1) * LANES)
            cur = src[pad:pad + n_chunk, ls]
            prev = src[pad - shift:pad - shift + n_chunk, ls]
            dst[pad:pad + n_chunk, ls] = cur + prev * m_a + pltpu.roll(prev, n, axis=1) * m_b
            mults[gl] = (m_re * m_re - m_im * m_im, 2.0 * m_re * m_im)
        src, dst = dst, src
        shift *= 2
    x_in = src[pad - 1:pad - 1 + n_chunk, :]
    y_ref[...] = y_intra + jnp.dot(x_in.astype(BF16), cout_sc[...], preferred_element_type=F32)


def _ssm(u_tiles, n_batch, seq, lam_re, lam_im, log_dt, b_re, b_im, c_re, c_im):
    g, p, n, cc, ng = SSM_G, SSM_P, SSM_N, SSM_CHUNK, SSM_TG
    n_tiles = g // ng
    n_chunk = seq // cc
    width = cc * LANES
    pad = max(n_chunk // 2, SUBLANES)
    u4 = u_tiles.reshape(n_tiles, n_batch, n_chunk, width)
    tau_p = np.arange(cc * p)
    tile_m = jnp.asarray((tau_p[None, :] % p == np.arange(p)[:, None]).astype(np.float32))
    spread_m = jnp.asarray((np.arange(width)[None, :] == ((tau_p // p) * LANES + tau_p % p)[:, None])
                           .astype(np.float32)).astype(BF16)
    params = [lam_re.reshape(g, 1, n), lam_im.reshape(g, 1, n), lam_re.reshape(g, n, 1),
              lam_im.reshape(g, n, 1), log_dt.reshape(g, 1, 1), b_re.transpose(0, 2, 1),
              b_im.transpose(0, 2, 1), c_re.transpose(0, 2, 1), c_im.transpose(0, 2, 1)]
    per_tile = lambda a: pl.BlockSpec((ng,) + a.shape[1:], lambda l, b: (l, 0, 0))
    const = lambda a: pl.BlockSpec(a.shape, lambda l, b: (0, 0))
    io_spec = pl.BlockSpec((None, None, n_chunk, width), lambda l, b: (l, b, 0, 0))
    y = pl.pallas_call(
        functools.partial(_ssm_kernel, n_chunk=n_chunk, pad=pad),
        grid=(n_tiles, n_batch),
        in_specs=[io_spec] + [per_tile(a) for a in params] + [const(tile_m), const(spread_m)],
        out_specs=io_spec,
        out_shape=jax.ShapeDtypeStruct(u4.shape, F32),
        scratch_shapes=[pltpu.VMEM((width, width), BF16), pltpu.VMEM((width, ng * LANES), BF16),
                        pltpu.VMEM((ng * LANES, width), BF16),
                        pltpu.VMEM((pad + n_chunk, ng * LANES), F32),
                        pltpu.VMEM((pad + n_chunk, ng * LANES), F32)],
        compiler_params=_cparams(("arbitrary", "arbitrary")),
        name="ssm",
    )(u4, *params, tile_m, spread_m)
    return y.reshape(n_tiles, n_batch * seq, LANES)


def _dsa_kernel(ki_ref, qit_ref, wt_ref, k_ref, qt_ref, vt_ref, slope_ref, o_ref,
                score_sc, qz_sc, acc_sc, m_sc, bias_sc, lg_sc, p_sc, *, n_sel):
    tq, tk = ATT_TQ, ATT_TK
    i = pl.program_id(1)
    nj = i + 1
    row = lax.broadcasted_iota(I32, (tk, tq), 0)
    col = lax.broadcasted_iota(I32, (tk, tq), 1)
    causal_diag = row <= col

    def fold8(x, op):
        return op(x.reshape(tk // SUBLANES, SUBLANES, tq), axis=0)

    def score_block(j, carry):
        lo8, hi8 = carry
        kib = ki_ref[j]
        s = jnp.zeros((tk, tq), F32)
        for h in range(IDX_HEADS):
            d = jnp.dot(kib, qit_ref[h * IDX_DIM:(h + 1) * IDX_DIM, :], preferred_element_type=F32)
            s = s + jnp.maximum(d, 0.0) * wt_ref[h:h + 1, :]
        causal = jnp.logical_or(j < i, causal_diag)
        score_sc[j] = jnp.where(causal, s, -jnp.inf)
        return (jnp.minimum(lo8, fold8(jnp.where(causal, s, jnp.inf), jnp.min)),
                jnp.maximum(hi8, fold8(jnp.where(causal, s, -jnp.inf), jnp.max)))

    lo8, hi8 = lax.fori_loop(0, nj, score_block, (jnp.full((SUBLANES, tq), jnp.inf, F32),
                                                  jnp.full((SUBLANES, tq), -jnp.inf, F32)))
    s_min = jnp.min(lo8, axis=0, keepdims=True)
    s_max = jnp.max(hi8, axis=0, keepdims=True)

    def count_ge(cand):
        def body(p, acc):
            j1 = jnp.minimum(2 * p + 1, nj - 1)
            w1 = jnp.where(2 * p + 1 < nj, 1.0, 0.0)
            return (acc + fold8(jnp.where(score_sc[2 * p] >= cand, 1.0, 0.0), jnp.sum)
                    + fold8(jnp.where(score_sc[j1] >= cand, w1, 0.0), jnp.sum))
        acc = lax.fori_loop(0, (nj + 1) // 2, body, jnp.zeros((SUBLANES, tq), F32))
        return acc.sum(axis=0, keepdims=True)

    def bisect_cond(state):
        it, n_open = state[0], state[1]
        return jnp.logical_and(it < BISECT_STEPS, n_open > 0)

    def bisect_step(state):
        it, _, open_q, lo, hi = state
        for _ in range(2):
            mid = lo + (hi - lo) * 0.5
            cnt = count_ge(mid)
            ok = cnt >= n_sel
            open_q = jnp.where(cnt == n_sel, 0, open_q)
            lo, hi = jnp.where(ok, mid, lo), jnp.where(ok, hi, mid)
        return it + 2, jnp.sum(open_q), open_q, lo, hi

    hi0 = s_max + jnp.abs(s_max) * 2.0 ** -20 + 1e-30
    state = (jnp.int32(0), jnp.int32(1), jnp.ones((1, tq), I32), s_min, hi0)
    thr_lo, thr_hi = lax.while_loop(bisect_cond, bisect_step, state)[3:]
    need_eq = n_sel - count_ge(thr_hi)

    for h in range(N_HEADS):
        qz_sc[h] = jnp.concatenate([qt_ref[h * HEAD_DIM:(h + 1) * HEAD_DIM, :], slope_ref[h]], axis=0)
    acc_sc[...] = jnp.zeros_like(acc_sc)
    m_sc[...] = jnp.full(m_sc.shape, NEG_BIG, F32)
    stril = (row > col).astype(BF16)
    n_chunk = 4
    rows_c = tk // n_chunk

    def attend_block(j, eq_seen):
        sb = score_sc[j]
        above = sb >= thr_hi
        eq = jnp.logical_and(sb >= thr_lo, jnp.logical_not(above))
        eq_f = jnp.where(eq, 1.0, 0.0).astype(BF16)
        before = jnp.dot(stril, eq_f, preferred_element_type=F32) + eq_seen
        sel = jnp.logical_or(above, jnp.logical_and(eq, before < need_eq))
        bias_sc[...] = jnp.where(sel, 0.0, -jnp.inf)
        block_max = []
        for h in range(N_HEADS):
            lg = jnp.dot(k_ref[j, :, h * LANES:(h + 1) * LANES], qz_sc[h],
                         preferred_element_type=F32) + bias_sc[...]
            lg_sc[h] = lg
            block_max.append(jnp.max(lg, axis=0, keepdims=True))
        m_old = m_sc[...]
        m_new = jnp.maximum(m_old, jnp.concatenate(block_max, axis=0))
        alpha = jnp.exp2(m_old - m_new)
        m_sc[...] = m_new
        for h in range(N_HEADS):
            for c in range(n_chunk):
                rs = slice(c * rows_c, (c + 1) * rows_c)
                p_sc[h, rs, :] = jnp.exp2(lg_sc[h, rs, :] - m_new[h:h + 1, :]).astype(BF16)
            pv = jnp.dot(vt_ref[j, h * ATT_VROWS:(h + 1) * ATT_VROWS, :], p_sc[h],
                         preferred_element_type=F32)
            sl = slice(h * ATT_VROWS, (h + 1) * ATT_VROWS)
            acc_sc[sl, :] = alpha[h:h + 1, :] * acc_sc[sl, :] + pv
        return eq_seen + jnp.sum(eq_f.astype(F32), axis=0, keepdims=True)

    lax.fori_loop(0, nj, attend_block, jnp.zeros((1, tq), F32))
    for h in range(N_HEADS):
        base = h * ATT_VROWS
        o_ref[h * HEAD_DIM:(h + 1) * HEAD_DIM, :] = (
            acc_sc[base:base + HEAD_DIM, :] / acc_sc[base + HEAD_DIM:base + HEAD_DIM + 1, :]).astype(o_ref.dtype)


def _alibi_rows(tq):
    rows = []
    for h in range(N_HEADS):
        rest = 2.0 ** (-8.0 * (h + 1) / N_HEADS) * Q_SCALE * HEAD_DIM ** 0.5
        pieces = []
        for _ in range(ALIBI_SPLIT):
            piece = float(np.asarray(rest, dtype=BF16).astype(np.float32))
            pieces.append(piece)
            rest -= piece
        col = pieces + [ATT_TK * x for x in pieces] + [0.0] * (HEAD_DIM - 2 * ALIBI_SPLIT)
        rows.append(col)
    return jnp.broadcast_to(jnp.asarray(np.asarray(rows, np.float32))[:, :, None],
                            (N_HEADS, HEAD_DIM, tq)).astype(BF16)


def _dsa(qt, k, vt, qit, ki, wt, n_batch, seq):
    tq, tk = ATT_TQ, ATT_TK
    nkb = seq // tk
    n_sel = min(TOPK_KEYS, seq // 4)
    kb = k.reshape(n_batch, nkb, tk, N_HEADS * LANES)
    kib = ki.reshape(n_batch, nkb, tk, IDX_DIM)
    per_batch = lambda shape: pl.BlockSpec((None,) + shape, lambda b, i: (b, 0, 0, 0),
                                           pipeline_mode=pl.Buffered(1))
    per_q = lambda r: pl.BlockSpec((None, r, tq), lambda b, i: (b, 0, i))
    return pl.pallas_call(
        functools.partial(_dsa_kernel, n_sel=n_sel),
        grid=(n_batch, seq // tq),
        in_specs=[per_batch((nkb, tk, IDX_DIM)), per_q(IDX_HEADS * IDX_DIM), per_q(IDX_HEADS),
                  per_batch((nkb, tk, N_HEADS * LANES)), per_q(ATTN_WIDTH),
                  per_batch((nkb, N_HEADS * ATT_VROWS, tk)),
                  pl.BlockSpec((N_HEADS, HEAD_DIM, tq), lambda b, i: (0, 0, 0))],
        out_specs=per_q(ATTN_WIDTH),
        out_shape=jax.ShapeDtypeStruct((n_batch, ATTN_WIDTH, seq), BF16),
        scratch_shapes=[pltpu.VMEM((nkb, tk, tq), F32), pltpu.VMEM((N_HEADS, LANES, tq), BF16),
                        pltpu.VMEM((N_HEADS * ATT_VROWS, tq), F32), pltpu.VMEM((N_HEADS, tq), F32),
                        pltpu.VMEM((tk, tq), F32), pltpu.VMEM((N_HEADS, tk, tq), F32),
                        pltpu.VMEM((N_HEADS, tk, tq), BF16)],
        compiler_params=_cparams(("parallel", "arbitrary")),
        name="dsa",
    )(kib, qit, wt, kb, qt, vt, _alibi_rows(tq))


def _merge_kernel(x_ref, g0_ref, b0_ref, ys_ref, u_ref, dsk_ref, att_ref, gp_ref, bg_ref,
                  wglu_ref, wap_ref, wo_ref, g1_ref, b1_ref, wr_ref, br_ref,
                  h1_ref, idx_ref, gate_ref):
    d = D_MODEL
    h = _layer_norm(x_ref[...], g0_ref[...], b0_ref[...])
    tiles = range(SSM_WIDTH // LANES)
    ys = jnp.concatenate([ys_ref[lt] for lt in tiles], axis=1)
    u = jnp.concatenate([u_ref[lt] for lt in tiles], axis=1)
    y = jax.nn.gelu(ys + dsk_ref[...] * u)
    glu = jnp.dot(y.astype(BF16), wglu_ref[...], preferred_element_type=F32)
    y_ssm = glu[:, :d] * jax.nn.sigmoid(glu[:, d:])
    y_att = lax.dot_general(att_ref[...], wap_ref[...], (((0,), (0,)), ((), ())),
                            preferred_element_type=F32)
    gs = jax.nn.sigmoid(gp_ref[...].astype(F32) + bg_ref[...])
    mixed = gs[:, :d] * y_ssm + gs[:, d:] * y_att
    mix = jnp.dot(mixed.astype(BF16), wo_ref[...], preferred_element_type=F32)
    h1 = _layer_norm(DEEPNORM_ALPHA * h + mix, g1_ref[...], b1_ref[...])
    h1_ref[...] = h1
    logits = jnp.dot(h1, wr_ref[...], precision=lax.Precision.HIGHEST,
                     preferred_element_type=F32) + br_ref[...]
    lane = lax.broadcasted_iota(I32, logits.shape, 1)
    logits = jnp.where(lane < N_EXPERTS, logits, -jnp.inf)
    idx_out = jnp.zeros(logits.shape, I32)
    val_out = jnp.zeros(logits.shape, F32)
    vals = []
    for kk in range(TOP_K):
        vmax = jnp.max(logits, axis=-1, keepdims=True)
        imax = jnp.min(jnp.where(logits == vmax, lane, LANES), axis=-1, keepdims=True)
        idx_out = jnp.where(lane == kk, imax, idx_out)
        vals.append(vmax)
        logits = jnp.where(lane == imax, -jnp.inf, logits)
    es = [jnp.exp(vk - vals[0]) for vk in vals]
    den = es[0] + es[1] + es[2] + es[3]
    for kk in range(TOP_K):
        val_out = jnp.where(lane == kk, es[kk] / den, val_out)
    idx_ref[...] = idx_out
    gate_ref[...] = val_out


def _merge(xt, ln0_g, ln0_b, ys, u, d_skip, att, gp, b_gate, w_glu, w_ap, w_o, ln1_g, ln1_b,
           w_router, b_router):
    t, d = xt.shape
    tm = 256
    wr = jnp.pad(w_router, ((0, 0), (0, LANES - N_EXPERTS)))
    br = jnp.pad(b_router, (0, LANES - N_EXPERTS)).reshape(1, LANES)
    ins = [xt, ln0_g.reshape(1, d), ln0_b.reshape(1, d), ys, u, d_skip.reshape(1, SSM_WIDTH), att, gp,
           b_gate.reshape(1, 2 * d), w_glu.astype(BF16), w_ap.astype(BF16), w_o.astype(BF16),
           ln1_g.reshape(1, d), ln1_b.reshape(1, d), wr, br]
    tiled = {0, 7}
    n_l = att.shape[2] // tm
    in_specs = []
    for n, a in enumerate(ins):
        if n in tiled:
            in_specs.append(pl.BlockSpec((tm, a.shape[1]), lambda i: (i, 0)))
        elif n in (3, 4):
            in_specs.append(pl.BlockSpec((a.shape[0], tm, LANES), lambda i: (0, i, 0)))
        elif n == 6:
            in_specs.append(pl.BlockSpec((None, ATTN_WIDTH, tm), lambda i: (i // n_l, 0, i % n_l)))
        else:
            in_specs.append(pl.BlockSpec(a.shape, lambda i: (0, 0)))
    return pl.pallas_call(
        _merge_kernel,
        grid=(t // tm,),
        in_specs=in_specs,
        out_specs=[pl.BlockSpec((tm, d), lambda i: (i, 0)), pl.BlockSpec((tm, LANES), lambda i: (i, 0)),
                   pl.BlockSpec((tm, LANES), lambda i: (i, 0))],
        out_shape=[jax.ShapeDtypeStruct((t, d), F32), jax.ShapeDtypeStruct((t, LANES), I32),
                   jax.ShapeDtypeStruct((t, LANES), F32)],
        compiler_params=_cparams(("parallel",)),
        name="merge",
    )(*ins)


def _rank_kernel(idx_ref, rank_ref, cnt_ref, run_sc):
    tm = idx_ref.shape[0]

    @pl.when(pl.program_id(0) == 0)
    def _():
        run_sc[...] = jnp.zeros_like(run_sc)

    idx = idx_ref[...]
    lane = lax.broadcasted_iota(I32, (tm, LANES), 1)
    hots = [lane == idx[:, kk:kk + 1] for kk in range(TOP_K)]
    multi = sum(hh.astype(F32) for hh in hots)
    r_i = lax.broadcasted_iota(I32, (tm, tm), 0)
    c_i = lax.broadcasted_iota(I32, (tm, tm), 1)
    before = jnp.dot((c_i < r_i).astype(BF16), multi.astype(BF16), preferred_element_type=F32)
    base = before + run_sc[...]
    out = jnp.zeros((tm, LANES), I32)
    for kk in range(TOP_K):
        rk = jnp.sum(jnp.where(hots[kk], base, 0.0), axis=-1, keepdims=True)
        out = jnp.where(lane == kk, rk.astype(I32), out)
    rank_ref[...] = out
    run_sc[...] = run_sc[...] + jnp.sum(multi, axis=0, keepdims=True)
    cnt_ref[...] = run_sc[...]


def _rank(idx):
    t = idx.shape[0]
    tm = 512
    return pl.pallas_call(
        _rank_kernel,
        grid=(t // tm,),
        in_specs=[pl.BlockSpec((tm, LANES), lambda i: (i, 0))],
        out_specs=[pl.BlockSpec((tm, LANES), lambda i: (i, 0)), pl.BlockSpec((1, LANES), lambda i: (0, 0))],
        out_shape=[jax.ShapeDtypeStruct((t, LANES), I32), jax.ShapeDtypeStruct((1, LANES), F32)],
        scratch_shapes=[pltpu.VMEM((1, LANES), F32)],
        compiler_params=_cparams(("arbitrary",)),
        name="moe_rank",
    )(idx)


def _row_copy(src_ref, src_row, dst_ref, dst_row, sem):
    return pltpu.make_async_copy(src_ref.at[pl.ds(src_row, 1), :], dst_ref.at[pl.ds(dst_row, 1), :], sem)


def _dispatch_kernel(dest_ref, pend_ref, h_ref, xs_ref, zero_sc, sem, zsem):
    tm = h_ref.shape[0]
    base = pl.program_id(0) * tm * TOP_K

    @pl.when(pl.program_id(0) == 0)
    def _():
        zero_sc[...] = jnp.zeros_like(zero_sc)

        def zero_copy(e):
            start = pl.multiple_of(pend_ref[e + 1] - MOE_ROWS, MOE_ROWS)
            return pltpu.make_async_copy(zero_sc, xs_ref.at[pl.ds(start, MOE_ROWS), :], zsem)

        for e in range(N_EXPERTS):
            @pl.when(pend_ref[e + 1] > pend_ref[e])
            def _():
                zero_copy(e).start()
        for e in range(N_EXPERTS):
            @pl.when(pend_ref[e + 1] > pend_ref[e])
            def _():
                zero_copy(e).wait()

        def tail_copy(blk):
            return pltpu.make_async_copy(
                zero_sc, xs_ref.at[pl.ds(pl.multiple_of(blk * MOE_ROWS, MOE_ROWS), MOE_ROWS), :], zsem)

        first_free = pend_ref[N_EXPERTS] // MOE_ROWS
        n_blocks = xs_ref.shape[0] // MOE_ROWS
        lax.fori_loop(first_free, n_blocks, lambda blk, c: (tail_copy(blk).start(), c)[1], 0)
        lax.fori_loop(first_free, n_blocks, lambda blk, c: (tail_copy(blk).wait(), c)[1], 0)

    def start(r, c):
        for kk in range(TOP_K):
            _row_copy(h_ref, r, xs_ref, dest_ref[base + r * TOP_K + kk], sem).start(priority=kk % 2)
        return c

    lax.fori_loop(0, tm, start, 0)

    def wait(r, c):
        for kk in range(TOP_K):
            _row_copy(h_ref, 0, xs_ref, 0, sem).wait()
        return c

    lax.fori_loop(0, tm, wait, 0)


def _dispatch(dest_flat, pend0, h1, n_rows):
    t, d = h1.shape
    tm = 512
    return pl.pallas_call(
        _dispatch_kernel,
        grid_spec=pltpu.PrefetchScalarGridSpec(
            num_scalar_prefetch=2,
            grid=(t // tm,),
            in_specs=[pl.BlockSpec((tm, d), lambda i, dest, pend: (i, 0))],
            out_specs=pl.BlockSpec(memory_space=pl.ANY),
            scratch_shapes=[pltpu.VMEM((MOE_ROWS, d), F32), pltpu.SemaphoreType.DMA(()),
                            pltpu.SemaphoreType.DMA(())]),
        out_shape=jax.ShapeDtypeStruct((n_rows, d), F32),
        compiler_params=_cparams(("arbitrary",)),
        name="moe_dispatch",
    )(dest_flat, pend0, h1)


def _ffn_kernel(be_ref, nused_ref, xs_ref, wup_ref, bup_ref, wdn_ref, bdn_ref, y_ref, wup_sc, wdn_sc):
    b = pl.program_id(0)
    used = b < nused_ref[0]

    @pl.when(jnp.logical_and(used, jnp.logical_or(b == 0, be_ref[b] != be_ref[jnp.maximum(b - 1, 0)])))
    def _():
        wup_sc[...] = wup_ref[...].astype(BF16)
        wdn_sc[...] = wdn_ref[...].astype(BF16)

    @pl.when(used)
    def _():
        x = xs_ref[...].astype(BF16)
        hdn = jnp.dot(x, wup_sc[...], preferred_element_type=F32) + bup_ref[...]
        h_gate = jnp.minimum(hdn[:, :D_FF], SWIGLU_LIMIT)
        h_lin = jnp.clip(hdn[:, D_FF:], -SWIGLU_LIMIT, SWIGLU_LIMIT)
        act = (h_lin + 1.0) * (h_gate * jax.nn.sigmoid(SWIGLU_ALPHA * h_gate))
        y_ref[...] = jnp.dot(act.astype(BF16), wdn_sc[...], preferred_element_type=F32) + bdn_ref[...]

    @pl.when(b >= nused_ref[0])
    def _():
        y_ref[...] = jnp.zeros_like(y_ref)


def _ffn(blk_expert, n_used, xs, w_up, b_up, w_down, b_down):
    n_rows, d = xs.shape
    r = MOE_ROWS
    e = N_EXPERTS
    return pl.pallas_call(
        _ffn_kernel,
        grid_spec=pltpu.PrefetchScalarGridSpec(
            num_scalar_prefetch=2,
            grid=(n_rows // r,),
            in_specs=[pl.BlockSpec((r, d), lambda b, be, nu: (jnp.minimum(b, nu[0] - 1), 0)),
                      pl.BlockSpec((None, d, 2 * D_FF), lambda b, be, nu: (be[b], 0, 0)),
                      pl.BlockSpec((None, 1, 2 * D_FF), lambda b, be, nu: (be[b], 0, 0)),
                      pl.BlockSpec((None, D_FF, d), lambda b, be, nu: (be[b], 0, 0)),
                      pl.BlockSpec((None, 1, d), lambda b, be, nu: (be[b], 0, 0))],
            out_specs=pl.BlockSpec((r, d), lambda b, be, nu: (b, 0)),
            scratch_shapes=[pltpu.VMEM((d, 2 * D_FF), BF16), pltpu.VMEM((D_FF, d), BF16)]),
        out_shape=jax.ShapeDtypeStruct((n_rows, d), F32),
        compiler_params=_cparams(("arbitrary",)),
        name="moe_ffn",
    )(blk_expert, n_used, xs, w_up, b_up.reshape(e, 1, 2 * D_FF), w_down, b_down.reshape(e, 1, d))


def _combine_kernel(dest_ref, h1_ref, gate_ref, g2_ref, b2_ref, y_ref, o_ref, buf, sem):
    tm = h1_ref.shape[0]
    step = pl.program_id(0)
    slot = step % 2

    def gather(tile):
        sl = tile % 2

        def start(r, c):
            for kk in range(TOP_K):
                _row_copy(y_ref, dest_ref[(tile * tm + r) * TOP_K + kk], buf.at[sl, kk], r,
                          sem.at[sl]).start(priority=kk % 2)
            return c

        lax.fori_loop(0, tm, start, 0)

    @pl.when(step == 0)
    def _():
        gather(0)

    @pl.when(step + 1 < pl.num_programs(0))
    def _():
        gather(step + 1)

    def wait(r, c):
        for kk in range(TOP_K):
            _row_copy(y_ref, 0, buf.at[slot, kk], 0, sem.at[slot]).wait()
        return c

    lax.fori_loop(0, tm, wait, 0)
    gates = gate_ref[...]
    ffn = jnp.zeros(h1_ref.shape, F32)
    for kk in range(TOP_K):
        ffn = ffn + buf[slot, kk] * gates[:, kk:kk + 1]
    o_ref[...] = _layer_norm(DEEPNORM_ALPHA * h1_ref[...] + ffn, g2_ref[...], b2_ref[...])


def _combine(dest_flat, h1, gates, ln2_g, ln2_b, y):
    t, d = h1.shape
    tm = 256
    return pl.pallas_call(
        _combine_kernel,
        grid_spec=pltpu.PrefetchScalarGridSpec(
            num_scalar_prefetch=1,
            grid=(t // tm,),
            in_specs=[pl.BlockSpec((tm, d), lambda i, dest: (i, 0)),
                      pl.BlockSpec((tm, LANES), lambda i, dest: (i, 0)),
                      pl.BlockSpec((1, d), lambda i, dest: (0, 0)),
                      pl.BlockSpec((1, d), lambda i, dest: (0, 0)),
                      pl.BlockSpec(memory_space=pl.ANY)],
            out_specs=pl.BlockSpec((tm, d), lambda i, dest: (i, 0)),
            scratch_shapes=[pltpu.VMEM((2, TOP_K, tm, d), F32), pltpu.SemaphoreType.DMA((2,))]),
        out_shape=jax.ShapeDtypeStruct((t, d), F32),
        compiler_params=_cparams(("arbitrary",)),
        name="moe_combine",
    )(dest_flat, h1, gates, ln2_g.reshape(1, d), ln2_b.reshape(1, d), y)


def kernel(x, ln_in_g, ln_in_b, w_in, b_gate, lam_re, lam_im, log_dt, b_re, b_im, c_re, c_im, d_skip,
           w_glu, w_attn_proj, w_o, ln1_g, ln1_b, w_router, b_router, w_up, b_up, w_down, b_down,
           ln2_g, ln2_b):
    n_batch, seq, d = x.shape
    assert d == D_MODEL and seq % ATT_TQ == 0 and w_in.shape[0] == DEPTH
    t = n_batch * seq
    xt = x.reshape(t, d)

    u, qt, k, vt, qit, ki, wt, gp = _inproj(xt, ln_in_g, ln_in_b, w_in[0], n_batch, seq)
    ys = _ssm(u, n_batch, seq, lam_re[0], lam_im[0], log_dt[0], b_re[0], b_im[0], c_re[0], c_im[0])
    att = _dsa(qt, k, vt, qit, ki, wt, n_batch, seq)
    h1, idx, gates = _merge(xt, ln_in_g, ln_in_b, ys, u, d_skip[0], att, gp, b_gate[0], w_glu[0],
                            w_attn_proj[0], w_o[0], ln1_g[0], ln1_b[0], w_router[0], b_router[0])

    rank, counts = _rank(idx)
    counts = counts[0, :N_EXPERTS].astype(I32)
    padded = (counts + MOE_ROWS - 1) // MOE_ROWS * MOE_ROWS
    pends = jnp.cumsum(padded)
    pstarts = pends - padded
    top_idx = idx[:, :TOP_K]
    dest = (pstarts[top_idx] + rank[:, :TOP_K]).reshape(-1)
    n_rows = t * TOP_K + N_EXPERTS * MOE_ROWS
    blk_start = jnp.arange(n_rows // MOE_ROWS, dtype=I32) * MOE_ROWS
    blk_expert = jnp.minimum(jnp.sum((pends[None, :] <= blk_start[:, None]).astype(I32), axis=1),
                             N_EXPERTS - 1)
    n_used = (pends[-1:] // MOE_ROWS).astype(I32)

    xs = _dispatch(dest, jnp.concatenate([jnp.zeros((1,), I32), pends]), h1, n_rows)
    y = _ffn(blk_expert, n_used, xs, w_up[0], b_up[0], w_down[0], b_down[0])
    out = _combine(dest, h1, gates, ln2_g[0], ln2_b[0], y)
    return out.reshape(n_batch, seq, d)
```

```python
import functools
import math

import jax
import jax.numpy as jnp
import numpy as np
from jax import lax
from jax.experimental import pallas as pl
from jax.experimental.pallas import tpu as pltpu

F32 = jnp.float32
BF16 = jnp.bfloat16
I32 = jnp.int32

D_MODEL = 1024
SSM_WIDTH = 512
SSM_P = 16
SSM_G = 32
SSM_N = 64
SSM_CHUNK = 16
SSM_TG = 128 // SSM_P
N_HEADS = 8
HEAD_DIM = 64
ATTN_WIDTH = 512
IDX_HEADS = 8
IDX_DIM = 64
TOPK_KEYS = 256
N_EXPERTS = 32
TOP_K = 4
D_FF = 1024
SWIGLU_LIMIT = 7.0
SWIGLU_ALPHA = 1.702
LN_EPS = 1e-5
DEPTH = 1
DEEPNORM_ALPHA = (2 * DEPTH) ** 0.25

LANES = 128
SUBLANES = 8
ATT_TQ = 256
ATT_TK = 256
ATT_VROWS = HEAD_DIM + 16
ALIBI_SPLIT = 4
MOE_ROWS = 256
BISECT_STEPS = 32
NEG_BIG = -1e30
Q_SCALE = float(np.float32(HEAD_DIM ** -0.5 * math.log2(math.e)))
VMEM_LIMIT = 56 * 1024 * 1024


def _cparams(sem, vmem=VMEM_LIMIT):
    return pltpu.CompilerParams(dimension_semantics=sem, vmem_limit_bytes=vmem)


def _layer_norm(x, g, b):
    mu = jnp.mean(x, axis=-1, keepdims=True)
    xc = x - mu
    var = jnp.mean(xc * xc, axis=-1, keepdims=True)
    return xc * lax.rsqrt(var + LN_EPS) * g + b


def _inproj_kernel(x_ref, g_ref, b_ref, wu_ref, wqt_ref, wk_ref, wvt_ref, wqit_ref, wsm_ref, wsmt_ref, wg_ref,
                   ones_ref, u_ref, qt_ref, k_ref, vt_ref, qit_ref, ki_ref, wt_ref, gp_ref, *, seq):
    h = _layer_norm(x_ref[...], g_ref[...], b_ref[...])
    hb = h.astype(BF16)
    tm = x_ref.shape[0]

    def proj(w_ref):
        return jnp.dot(hb, w_ref[...], preferred_element_type=F32)

    def proj_t(w_ref):
        return lax.dot_general(w_ref[...], hb, (((1,), (1,)), ((), ())), preferred_element_type=F32)

    u = proj(wu_ref)
    for lt in range(SSM_WIDTH // LANES):
        u_ref[lt] = u[:, lt * LANES:(lt + 1) * LANES]
    qt_ref[...] = (proj_t(wqt_ref) * Q_SCALE).astype(BF16)
    shape = (tm, N_HEADS * LANES)
    pos = (pl.program_id(0) * tm + lax.broadcasted_iota(I32, shape, 0)) % seq
    sub = lax.broadcasted_iota(I32, shape, 1) % LANES - HEAD_DIM
    feat = jnp.where(jnp.logical_and(sub >= 0, sub < ALIBI_SPLIT), pos % ATT_TK,
                     jnp.where(jnp.logical_and(sub >= ALIBI_SPLIT, sub < 2 * ALIBI_SPLIT), pos // ATT_TK, 0))
    k_ref[...] = (proj(wk_ref) + feat.astype(F32)).astype(BF16)
    vt = (proj_t(wvt_ref) + ones_ref[...]).astype(BF16)
    for kb in range(tm // ATT_TK):
        vt_ref[kb] = vt[:, kb * ATT_TK:(kb + 1) * ATT_TK]
    qit_ref[...] = proj_t(wqit_ref).astype(BF16)
    ki_ref[...] = proj(wsm_ref)[:, :IDX_DIM].astype(BF16)
    wt_ref[...] = proj_t(wsmt_ref)[IDX_DIM:IDX_DIM + IDX_HEADS, :] * (IDX_DIM ** -0.5 * IDX_HEADS ** -0.5)
    gp_ref[...] = proj(wg_ref).astype(BF16)


def _inproj(xt, ln_g, ln_b, w_in, n_batch, seq):
    t, d = xt.shape
    tm = 512
    n_l = seq // tm
    o = 0
    ws = []
    for n in (SSM_WIDTH, ATTN_WIDTH, ATTN_WIDTH, ATTN_WIDTH, IDX_HEADS * IDX_DIM):
        ws.append(w_in[:, o:o + n].astype(BF16))
        o += n
    w_u, w_q, w_k, w_v, w_qi = ws
    w_k = jnp.pad(w_k.reshape(d, N_HEADS, HEAD_DIM),
                  ((0, 0), (0, 0), (0, LANES - HEAD_DIM))).reshape(d, N_HEADS * LANES)
    w_vt = jnp.pad(w_v.T.reshape(N_HEADS, HEAD_DIM, d),
                   ((0, 0), (0, ATT_VROWS - HEAD_DIM), (0, 0))).reshape(N_HEADS * ATT_VROWS, d)
    ones_col = jnp.zeros((N_HEADS, ATT_VROWS, 1), F32).at[:, HEAD_DIM, 0].set(1.0).reshape(-1, 1)
    n_small = IDX_DIM + IDX_HEADS
    w_small = jnp.pad(w_in[:, o:o + n_small], ((0, 0), (0, LANES - n_small))).astype(BF16)
    o += n_small
    w_gate = w_in[:, o:].astype(BF16)
    full = lambda a: pl.BlockSpec(a.shape, lambda i: (0,) * a.ndim)
    row = lambda n: pl.BlockSpec((tm, n), lambda i: (i, 0))
    col = lambda r: pl.BlockSpec((None, r, tm), lambda i: (i // n_l, 0, i % n_l))
    ins = [xt, ln_g.reshape(1, d), ln_b.reshape(1, d), w_u, w_q.T, w_k, w_vt, w_qi.T, w_small, w_small.T,
           w_gate, ones_col]
    nkb = seq // ATT_TK
    out_specs = [pl.BlockSpec((SSM_WIDTH // LANES, tm, LANES), lambda i: (0, i, 0)),
                 col(ATTN_WIDTH), row(N_HEADS * LANES),
                 pl.BlockSpec((None, tm // ATT_TK, N_HEADS * ATT_VROWS, ATT_TK),
                              lambda i: (i // n_l, i % n_l, 0, 0)),
                 col(IDX_HEADS * IDX_DIM), row(IDX_DIM), col(IDX_HEADS), row(2 * d)]
    out_shape = [jax.ShapeDtypeStruct((SSM_WIDTH // LANES, t, LANES), F32),
                 jax.ShapeDtypeStruct((n_batch, ATTN_WIDTH, seq), BF16),
                 jax.ShapeDtypeStruct((t, N_HEADS * LANES), BF16),
                 jax.ShapeDtypeStruct((n_batch, nkb, N_HEADS * ATT_VROWS, ATT_TK), BF16),
                 jax.ShapeDtypeStruct((n_batch, IDX_HEADS * IDX_DIM, seq), BF16),
                 jax.ShapeDtypeStruct((t, IDX_DIM), BF16),
                 jax.ShapeDtypeStruct((n_batch, IDX_HEADS, seq), F32),
                 jax.ShapeDtypeStruct((t, 2 * d), BF16)]
    return pl.pallas_call(
        functools.partial(_inproj_kernel, seq=seq),
        grid=(t // tm,),
        in_specs=[row(d)] + [full(a) for a in ins[1:]],
        out_specs=out_specs,
        out_shape=out_shape,
        compiler_params=_cparams(("parallel",)),
        name="inproj",
    )(*ins)


def _ssm_kernel(u_ref, lrr_ref, lir_ref, lrc_ref, lic_ref, ldt_ref, btr_ref, bti_ref, ctr_ref, cti_ref,
                tile_ref, spread_ref, y_ref, tg_sc, sin_sc, cout_sc, buf_a, buf_b, *, n_chunk, pad):
    cc, p, n, ng = SSM_CHUNK, SSM_P, SSM_N, SSM_TG
    hi = lax.Precision.HIGHEST
    width = cc * LANES

    def discretise(gl):
        dt = jnp.exp(ldt_ref[gl])
        return lrr_ref[gl] * dt, lir_ref[gl] * dt

    @pl.when(pl.program_id(1) == 0)
    def _():
        sin_sc[...] = jnp.zeros_like(sin_sc)
        w_rows = []
        for gl in range(ng):
            ar, ai = discretise(gl)
            mag = jnp.exp(ar)
            a_re, a_im = mag * jnp.cos(ai), mag * jnp.sin(ai)
            lr, li = lrr_ref[gl], lir_ref[gl]
            den = lr * lr + li * li
            f_re = ((a_re - 1.0) * lr + a_im * li) / den
            f_im = (a_im * lr - (a_re - 1.0) * li) / den
            bb_re = f_re * btr_ref[gl] - f_im * bti_ref[gl]
            bb_im = f_re * bti_ref[gl] + f_im * btr_ref[gl]
            dtc = jnp.exp(ldt_ref[gl])
            arc, aic = lrc_ref[gl] * dtc, lic_ref[gl] * dtc
            tau = (lax.broadcasted_iota(I32, (1, cc * p), 1) // p).astype(F32)
            rmag = jnp.exp(arc * tau)
            pw_re, pw_im = rmag * jnp.cos(aic * tau), rmag * jnp.sin(aic * tau)
            ct_re = jnp.dot(ctr_ref[gl], tile_ref[...], precision=hi, preferred_element_type=F32)
            ct_im = jnp.dot(cti_ref[gl], tile_ref[...], precision=hi, preferred_element_type=F32)
            r_re = pw_re * ct_re - pw_im * ct_im
            r_im = pw_re * ct_im + pw_im * ct_re
            m0 = (jnp.dot(bb_re, r_re, precision=hi, preferred_element_type=F32)
                  - jnp.dot(bb_im, r_im, precision=hi, preferred_element_type=F32))
            spread = jnp.dot(m0.astype(BF16), spread_ref[...], preferred_element_type=F32)
            w_rows.append(spread if gl == 0 else pltpu.roll(spread, gl * p, axis=1))
            e_pow = (cc - 1 - lax.broadcasted_iota(I32, (cc, 1), 0)).astype(F32)
            pmag = jnp.exp(ar * e_pow)
            ap_re, ap_im = pmag * jnp.cos(ai * e_pow), pmag * jnp.sin(ai * e_pow)
            for s in range(cc):
                pr, pi = ap_re[s:s + 1, :], ap_im[s:s + 1, :]
                sin_sc[s * LANES + gl * p:s * LANES + (gl + 1) * p, gl * LANES:(gl + 1) * LANES] = (
                    jnp.concatenate([bb_re * pr - bb_im * pi, bb_re * pi + bb_im * pr], axis=1).astype(BF16))
            mc = jnp.exp(arc)
            ac_re, ac_im = mc * jnp.cos(aic), mc * jnp.sin(aic)
            r1_re = ac_re * r_re - ac_im * r_im
            r1_im = ac_re * r_im + ac_im * r_re
            cout = jnp.concatenate([r1_re, -r1_im], axis=0).astype(BF16)
            cspread = jnp.dot(cout, spread_ref[...], preferred_element_type=F32)
            cout_sc[gl * LANES:(gl + 1) * LANES, :] = (
                cspread if gl == 0 else pltpu.roll(cspread, gl * p, axis=1)).astype(BF16)
        wb = jnp.concatenate(w_rows, axis=0).astype(BF16)
        for s in range(cc):
            tg_sc[s * LANES:(s + 1) * LANES, :] = wb if s == 0 else jnp.concatenate(
                [jnp.zeros((LANES, s * LANES), BF16), wb[:, :(cc - s) * LANES]], axis=1)

    @pl.when(jnp.logical_and(pl.program_id(0) == 0, pl.program_id(1) == 0))
    def _():
        buf_a[0:pad, :] = jnp.zeros((pad, ng * LANES), F32)
        buf_b[0:pad, :] = jnp.zeros((pad, ng * LANES), F32)

    u = u_ref[...].astype(BF16)
    y_intra = jnp.dot(u, tg_sc[...], preferred_element_type=F32)
    buf_a[pad:pad + n_chunk, :] = jnp.dot(u, sin_sc[...], preferred_element_type=F32)

    mults = []
    for gl in range(ng):
        ar, ai = discretise(gl)
        cm = jnp.exp(ar * cc)
        mults.append((cm * jnp.cos(ai * cc), cm * jnp.sin(ai * cc)))
    src, dst = buf_a, buf_b
    shift = 1
    while shift < n_chunk:
        for gl in range(ng):
            m_re, m_im = mults[gl]
            m_a = jnp.concatenate([m_re, m_re], axis=1)
            m_b = jnp.concatenate([-m_im, m_im], axis=1)
            ls = slice(gl * LANES, (gl + 1) * LANES)
            cur = src[pad:pad + n_chunk, ls]
            prev = src[pad - shift:pad - shift + n_chunk, ls]
            dst[pad:pad + n_chunk, ls] = cur + prev * m_a + pltpu.roll(prev, n, axis=1) * m_b
            mults[gl] = (m_re * m_re - m_im * m_im, 2.0 * m_re * m_im)
        src, dst = dst, src
        shift *= 2
    x_in = src[pad - 1:pad - 1 + n_chunk, :]
    y_ref[...] = y_intra + jnp.dot(x_in.astype(BF16), cout_sc[...], preferred_element_type=F32)


def _ssm(u_tiles, n_batch, seq, lam_re, lam_im, log_dt, b_re, b_im, c_re, c_im):
    g, p, n, cc, ng = SSM_G, SSM_P, SSM_N, SSM_CHUNK, SSM_TG
    n_tiles = g // ng
    n_chunk = seq // cc
    width = cc * LANES
    pad = max(n_chunk // 2, SUBLANES)
    u4 = u_tiles.reshape(n_tiles, n_batch, n_chunk, width)
    tau_p = np.arange(cc * p)
    tile_m = jnp.asarray((tau_p[None, :] % p == np.arange(p)[:, None]).astype(np.float32))
    spread_m = jnp.asarray((np.arange(width)[None, :] == ((tau_p // p) * LANES + tau_p % p)[:, None])
                           .astype(np.float32)).astype(BF16)
    params = [lam_re.reshape(g, 1, n), lam_im.reshape(g, 1, n), lam_re.reshape(g, n, 1),
              lam_im.reshape(g, n, 1), log_dt.reshape(g, 1, 1), b_re.transpose(0, 2, 1),
              b_im.transpose(0, 2, 1), c_re.transpose(0, 2, 1), c_im.transpose(0, 2, 1)]
    per_tile = lambda a: pl.BlockSpec((ng,) + a.shape[1:], lambda l, b: (l, 0, 0))
    const = lambda a: pl.BlockSpec(a.shape, lambda l, b: (0, 0))
    io_spec = pl.BlockSpec((None, None, n_chunk, width), lambda l, b: (l, b, 0, 0))
    y = pl.pallas_call(
        functools.partial(_ssm_kernel, n_chunk=n_chunk, pad=pad),
        grid=(n_tiles, n_batch),
        in_specs=[io_spec] + [per_tile(a) for a in params] + [const(tile_m), const(spread_m)],
        out_specs=io_spec,
        out_shape=jax.ShapeDtypeStruct(u4.shape, F32),
        scratch_shapes=[pltpu.VMEM((width, width), BF16), pltpu.VMEM((width, ng * LANES), BF16),
                        pltpu.VMEM((ng * LANES, width), BF16),
                        pltpu.VMEM((pad + n_chunk, ng * LANES), F32),
                        pltpu.VMEM((pad + n_chunk, ng * LANES), F32)],
        compiler_params=_cparams(("arbitrary", "arbitrary")),
        name="ssm",
    )(u4, *params, tile_m, spread_m)
    return y.reshape(n_tiles, n_batch * seq, LANES)


def _dsa_kernel(ki_ref, qit_ref, wt_ref, k_ref, qt_ref, vt_ref, slope_ref, o_ref,
                score_sc, qz_sc, acc_sc, m_sc, bias_sc, lg_sc, p_sc, *, n_sel):
    tq, tk = ATT_TQ, ATT_TK
    i = pl.program_id(1)
    nj = i + 1
    row = lax.broadcasted_iota(I32, (tk, tq), 0)
    col = lax.broadcasted_iota(I32, (tk, tq), 1)
    causal_diag = row <= col

    def fold8(x, op):
        return op(x.reshape(tk // SUBLANES, SUBLANES, tq), axis=0)

    def score_block(j, carry):
        lo8, hi8 = carry
        kib = ki_ref[j]
        s = jnp.zeros((tk, tq), F32)
        for h in range(IDX_HEADS):
            d = jnp.dot(kib, qit_ref[h * IDX_DIM:(h + 1) * IDX_DIM, :], preferred_element_type=F32)
            s = s + jnp.maximum(d, 0.0) * wt_ref[h:h + 1, :]
        causal = jnp.logical_or(j < i, causal_diag)
        score_sc[j] = jnp.where(causal, s, -jnp.inf)
        return (jnp.minimum(lo8, fold8(jnp.where(causal, s, jnp.inf), jnp.min)),
                jnp.maximum(hi8, fold8(jnp.where(causal, s, -jnp.inf), jnp.max)))

    lo8, hi8 = lax.fori_loop(0, nj, score_block, (jnp.full((SUBLANES, tq), jnp.inf, F32),
                                                  jnp.full((SUBLANES, tq), -jnp.inf, F32)))
    s_min = jnp.min(lo8, axis=0, keepdims=True)
    s_max = jnp.max(hi8, axis=0, keepdims=True)

    def count_ge(cand):
        def body(p, acc):
            j1 = jnp.minimum(2 * p + 1, nj - 1)
            w1 = jnp.where(2 * p + 1 < nj, 1.0, 0.0)
            return (acc + fold8(jnp.where(score_sc[2 * p] >= cand, 1.0, 0.0), jnp.sum)
                    + fold8(jnp.where(score_sc[j1] >= cand, w1, 0.0), jnp.sum))
        acc = lax.fori_loop(0, (nj + 1) // 2, body, jnp.zeros((SUBLANES, tq), F32))
        return acc.sum(axis=0, keepdims=True)

    def bisect_cond(state):
        it, n_open = state[0], state[1]
        return jnp.logical_and(it < BISECT_STEPS, n_open > 0)

    def bisect_step(state):
        it, _, open_q, lo, hi = state
        for _ in range(2):
            mid = lo + (hi - lo) * 0.5
            cnt = count_ge(mid)
            ok = cnt >= n_sel
            open_q = jnp.where(cnt == n_sel, 0, open_q)
            lo, hi = jnp.where(ok, mid, lo), jnp.where(ok, hi, mid)
        return it + 2, jnp.sum(open_q), open_q, lo, hi

    hi0 = s_max + jnp.abs(s_max) * 2.0 ** -20 + 1e-30
    state = (jnp.int32(0), jnp.int32(1), jnp.ones((1, tq), I32), s_min, hi0)
    thr_lo, thr_hi = lax.while_loop(bisect_cond, bisect_step, state)[3:]
    need_eq = n_sel - count_ge(thr_hi)

    for h in range(N_HEADS):
        qz_sc[h] = jnp.concatenate([qt_ref[h * HEAD_DIM:(h + 1) * HEAD_DIM, :], slope_ref[h]], axis=0)
    acc_sc[...] = jnp.zeros_like(acc_sc)
    m_sc[...] = jnp.full(m_sc.shape, NEG_BIG, F32)
    stril = (row > col).astype(BF16)
    n_chunk = 4
    rows_c = tk // n_chunk

    def attend_block(j, eq_seen):
        sb = score_sc[j]
        above = sb >= thr_hi
        eq = jnp.logical_and(sb >= thr_lo, jnp.logical_not(above))
        eq_f = jnp.where(eq, 1.0, 0.0).astype(BF16)
        before = jnp.dot(stril, eq_f, preferred_element_type=F32) + eq_seen
        sel = jnp.logical_or(above, jnp.logical_and(eq, before < need_eq))
        bias_sc[...] = jnp.where(sel, 0.0, -jnp.inf)
        block_max = []
        for h in range(N_HEADS):
            lg = jnp.dot(k_ref[j, :, h * LANES:(h + 1) * LANES], qz_sc[h],
                         preferred_element_type=F32) + bias_sc[...]
            lg_sc[h] = lg
            block_max.append(jnp.max(lg, axis=0, keepdims=True))
        m_old = m_sc[...]
        m_new = jnp.maximum(m_old, jnp.concatenate(block_max, axis=0))
        alpha = jnp.exp2(m_old - m_new)
        m_sc[...] = m_new
        for h in range(N_HEADS):
            for c in range(n_chunk):
                rs = slice(c * rows_c, (c + 1) * rows_c)
                p_sc[h, rs, :] = jnp.exp2(lg_sc[h, rs, :] - m_new[h:h + 1, :]).astype(BF16)
            pv = jnp.dot(vt_ref[j, h * ATT_VROWS:(h + 1) * ATT_VROWS, :], p_sc[h],
                         preferred_element_type=F32)
            sl = slice(h * ATT_VROWS, (h + 1) * ATT_VROWS)
            acc_sc[sl, :] = alpha[h:h + 1, :] * acc_sc[sl, :] + pv
        return eq_seen + jnp.sum(eq_f.astype(F32), axis=0, keepdims=True)

    lax.fori_loop(0, nj, attend_block, jnp.zeros((1, tq), F32))
    for h in range(N_HEADS):
        base = h * ATT_VROWS
        o_ref[h * HEAD_DIM:(h + 1) * HEAD_DIM, :] = (
            acc_sc[base:base + HEAD_DIM, :] / acc_sc[base + HEAD_DIM:base + HEAD_DIM + 1, :]).astype(o_ref.dtype)


def _alibi_rows(tq):
    rows = []
    for h in range(N_HEADS):
        rest = 2.0 ** (-8.0 * (h + 1) / N_HEADS) * Q_SCALE * HEAD_DIM ** 0.5
        pieces = []
        for _ in range(ALIBI_SPLIT):
            piece = float(np.asarray(rest, dtype=BF16).astype(np.float32))
            pieces.append(piece)
            rest -= piece
        col = pieces + [ATT_TK * x for x in pieces] + [0.0] * (HEAD_DIM - 2 * ALIBI_SPLIT)
        rows.append(col)
    return jnp.broadcast_to(jnp.asarray(np.asarray(rows, np.float32))[:, :, None],
                            (N_HEADS, HEAD_DIM, tq)).astype(BF16)


def _dsa(qt, k, vt, qit, ki, wt, n_batch, seq):
    tq, tk = ATT_TQ, ATT_TK
    nkb = seq // tk
    n_sel = min(TOPK_KEYS, seq // 4)
    kb = k.reshape(n_batch, nkb, tk, N_HEADS * LANES)
    kib = ki.reshape(n_batch, nkb, tk, IDX_DIM)
    per_batch = lambda shape: pl.BlockSpec((None,) + shape, lambda b, i: (b, 0, 0, 0),
                                           pipeline_mode=pl.Buffered(1))
    per_q = lambda r: pl.BlockSpec((None, r, tq), lambda b, i: (b, 0, i))
    return pl.pallas_call(
        functools.partial(_dsa_kernel, n_sel=n_sel),
        grid=(n_batch, seq // tq),
        in_specs=[per_batch((nkb, tk, IDX_DIM)), per_q(IDX_HEADS * IDX_DIM), per_q(IDX_HEADS),
                  per_batch((nkb, tk, N_HEADS * LANES)), per_q(ATTN_WIDTH),
                  per_batch((nkb, N_HEADS * ATT_VROWS, tk)),
                  pl.BlockSpec((N_HEADS, HEAD_DIM, tq), lambda b, i: (0, 0, 0))],
        out_specs=per_q(ATTN_WIDTH),
        out_shape=jax.ShapeDtypeStruct((n_batch, ATTN_WIDTH, seq), BF16),
        scratch_shapes=[pltpu.VMEM((nkb, tk, tq), F32), pltpu.VMEM((N_HEADS, LANES, tq), BF16),
                        pltpu.VMEM((N_HEADS * ATT_VROWS, tq), F32), pltpu.VMEM((N_HEADS, tq), F32),
                        pltpu.VMEM((tk, tq), F32), pltpu.VMEM((N_HEADS, tk, tq), F32),
                        pltpu.VMEM((N_HEADS, tk, tq), BF16)],
        compiler_params=_cparams(("parallel", "arbitrary")),
        name="dsa",
    )(kib, qit, wt, kb, qt, vt, _alibi_rows(tq))


def _merge_kernel(x_ref, g0_ref, b0_ref, ys_ref, u_ref, dsk_ref, att_ref, gp_ref, bg_ref,
                  wglu_ref, wap_ref, wo_ref, g1_ref, b1_ref, wrh_ref, wrl_ref, br_ref,
                  h1_ref, idx_ref, gate_ref):
    d = D_MODEL
    h = _layer_norm(x_ref[...], g0_ref[...], b0_ref[...])
    tiles = range(SSM_WIDTH // LANES)
    ys = jnp.concatenate([ys_ref[lt] for lt in tiles], axis=1)
    u = jnp.concatenate([u_ref[lt] for lt in tiles], axis=1)
    y = jax.nn.gelu(ys + dsk_ref[...] * u)
    glu = jnp.dot(y.astype(BF16), wglu_ref[...], preferred_element_type=F32)
    y_ssm = glu[:, :d] * jax.nn.sigmoid(glu[:, d:])
    y_att = lax.dot_general(att_ref[...], wap_ref[...], (((0,), (0,)), ((), ())),
                            preferred_element_type=F32)
    gs = jax.nn.sigmoid(gp_ref[...].astype(F32) + bg_ref[...])
    mixed = gs[:, :d] * y_ssm + gs[:, d:] * y_att
    mix = jnp.dot(mixed.astype(BF16), wo_ref[...], preferred_element_type=F32)
    h1 = _layer_norm(DEEPNORM_ALPHA * h + mix, g1_ref[...], b1_ref[...])
    h1_ref[...] = h1
    h1_hi = h1.astype(BF16)
    h1_lo = (h1 - h1_hi.astype(F32)).astype(BF16)
    logits = (jnp.dot(h1_hi, wrh_ref[...], preferred_element_type=F32)
              + jnp.dot(h1_hi, wrl_ref[...], preferred_element_type=F32)
              + jnp.dot(h1_lo, wrh_ref[...], preferred_element_type=F32)) + br_ref[...]
    lane = lax.broadcasted_iota(I32, logits.shape, 1)
    logits = jnp.where(lane < N_EXPERTS, logits, -jnp.inf)
    idx_out = jnp.zeros(logits.shape, I32)
    val_out = jnp.zeros(logits.shape, F32)
    vals = []
    for kk in range(TOP_K):
        vmax = jnp.max(logits, axis=-1, keepdims=True)
        imax = jnp.min(jnp.where(logits == vmax, lane, LANES), axis=-1, keepdims=True)
        idx_out = jnp.where(lane == kk, imax, idx_out)
        vals.append(vmax)
        logits = jnp.where(lane == imax, -jnp.inf, logits)
    es = [jnp.exp(vk - vals[0]) for vk in vals]
    den = es[0] + es[1] + es[2] + es[3]
    for kk in range(TOP_K):
        val_out = jnp.where(lane == kk, es[kk] / den, val_out)
    idx_ref[...] = idx_out
    gate_ref[...] = val_out


def _merge(xt, ln0_g, ln0_b, ys, u, d_skip, att, gp, b_gate, w_glu, w_ap, w_o, ln1_g, ln1_b,
           w_router, b_router):
    t, d = xt.shape
    tm = 256
    wr = jnp.pad(w_router, ((0, 0), (0, LANES - N_EXPERTS)))
    wr_hi = wr.astype(BF16)
    wr_lo = (wr - wr_hi.astype(F32)).astype(BF16)
    br = jnp.pad(b_router, (0, LANES - N_EXPERTS)).reshape(1, LANES)
    ins = [xt, ln0_g.reshape(1, d), ln0_b.reshape(1, d), ys, u, d_skip.reshape(1, SSM_WIDTH), att, gp,
           b_gate.reshape(1, 2 * d), w_glu.astype(BF16), w_ap.astype(BF16), w_o.astype(BF16),
           ln1_g.reshape(1, d), ln1_b.reshape(1, d), wr_hi, wr_lo, br]
    tiled = {0, 7}
    n_l = att.shape[2] // tm
    in_specs = []
    for n, a in enumerate(ins):
        if n in tiled:
            in_specs.append(pl.BlockSpec((tm, a.shape[1]), lambda i: (i, 0)))
        elif n in (3, 4):
            in_specs.append(pl.BlockSpec((a.shape[0], tm, LANES), lambda i: (0, i, 0)))
        elif n == 6:
            in_specs.append(pl.BlockSpec((None, ATTN_WIDTH, tm), lambda i: (i // n_l, 0, i % n_l)))
        else:
            in_specs.append(pl.BlockSpec(a.shape, lambda i: (0, 0)))
    return pl.pallas_call(
        _merge_kernel,
        grid=(t // tm,),
        in_specs=in_specs,
        out_specs=[pl.BlockSpec((tm, d), lambda i: (i, 0)), pl.BlockSpec((tm, LANES), lambda i: (i, 0)),
                   pl.BlockSpec((tm, LANES), lambda i: (i, 0))],
        out_shape=[jax.ShapeDtypeStruct((t, d), F32), jax.ShapeDtypeStruct((t, LANES), I32),
                   jax.ShapeDtypeStruct((t, LANES), F32)],
        compiler_params=_cparams(("parallel",)),
        name="merge",
    )(*ins)


def _rank_kernel(idx_ref, rank_ref, cnt_ref, run_sc):
    tm = idx_ref.shape[0]

    @pl.when(pl.program_id(0) == 0)
    def _():
        run_sc[...] = jnp.zeros_like(run_sc)

    idx = idx_ref[...]
    lane = lax.broadcasted_iota(I32, (tm, LANES), 1)
    hots = [lane == idx[:, kk:kk + 1] for kk in range(TOP_K)]
    multi = sum(hh.astype(F32) for hh in hots)
    r_i = lax.broadcasted_iota(I32, (tm, tm), 0)
    c_i = lax.broadcasted_iota(I32, (tm, tm), 1)
    before = jnp.dot((c_i < r_i).astype(BF16), multi.astype(BF16), preferred_element_type=F32)
    base = before + run_sc[...]
    out = jnp.zeros((tm, LANES), I32)
    for kk in range(TOP_K):
        rk = jnp.sum(jnp.where(hots[kk], base, 0.0), axis=-1, keepdims=True)
        out = jnp.where(lane == kk, rk.astype(I32), out)
    rank_ref[...] = out
    run_sc[...] = run_sc[...] + jnp.sum(multi, axis=0, keepdims=True)
    cnt_ref[...] = run_sc[...]


def _rank(idx):
    t = idx.shape[0]
    tm = 512
    return pl.pallas_call(
        _rank_kernel,
        grid=(t // tm,),
        in_specs=[pl.BlockSpec((tm, LANES), lambda i: (i, 0))],
        out_specs=[pl.BlockSpec((tm, LANES), lambda i: (i, 0)), pl.BlockSpec((1, LANES), lambda i: (0, 0))],
        out_shape=[jax.ShapeDtypeStruct((t, LANES), I32), jax.ShapeDtypeStruct((1, LANES), F32)],
        scratch_shapes=[pltpu.VMEM((1, LANES), F32)],
        compiler_params=_cparams(("arbitrary",)),
        name="moe_rank",
    )(idx)


def _row_copy(src_ref, src_row, dst_ref, dst_row, sem):
    return pltpu.make_async_copy(src_ref.at[pl.ds(src_row, 1), :], dst_ref.at[pl.ds(dst_row, 1), :], sem)


def _dispatch_kernel(dest_ref, pend_ref, h_ref, xs_ref, zero_sc, sem, zsem):
    tm = h_ref.shape[0]
    base = pl.program_id(0) * tm * TOP_K

    @pl.when(pl.program_id(0) == 0)
    def _():
        zero_sc[...] = jnp.zeros_like(zero_sc)

        def zero_copy(e):
            start = pl.multiple_of(pend_ref[e + 1] - MOE_ROWS, MOE_ROWS)
            return pltpu.make_async_copy(zero_sc, xs_ref.at[pl.ds(start, MOE_ROWS), :], zsem)

        for e in range(N_EXPERTS):
            @pl.when(pend_ref[e + 1] > pend_ref[e])
            def _():
                zero_copy(e).start()
        for e in range(N_EXPERTS):
            @pl.when(pend_ref[e + 1] > pend_ref[e])
            def _():
                zero_copy(e).wait()

        def tail_copy(blk):
            return pltpu.make_async_copy(
                zero_sc, xs_ref.at[pl.ds(pl.multiple_of(blk * MOE_ROWS, MOE_ROWS), MOE_ROWS), :], zsem)

        first_free = pend_ref[N_EXPERTS] // MOE_ROWS
        n_blocks = xs_ref.shape[0] // MOE_ROWS
        lax.fori_loop(first_free, n_blocks, lambda blk, c: (tail_copy(blk).start(), c)[1], 0)
        lax.fori_loop(first_free, n_blocks, lambda blk, c: (tail_copy(blk).wait(), c)[1], 0)

    def start(r, c):
        for kk in range(TOP_K):
            _row_copy(h_ref, r, xs_ref, dest_ref[base + r * TOP_K + kk], sem).start()
        return c

    lax.fori_loop(0, tm, start, 0)

    def wait(r, c):
        for kk in range(TOP_K):
            _row_copy(h_ref, 0, xs_ref, 0, sem).wait()
        return c

    lax.fori_loop(0, tm, wait, 0)


def _dispatch(dest_flat, pend0, h1, n_rows):
    t, d = h1.shape
    tm = 512
    return pl.pallas_call(
        _dispatch_kernel,
        grid_spec=pltpu.PrefetchScalarGridSpec(
            num_scalar_prefetch=2,
            grid=(t // tm,),
            in_specs=[pl.BlockSpec((tm, d), lambda i, dest, pend: (i, 0))],
            out_specs=pl.BlockSpec(memory_space=pl.ANY),
            scratch_shapes=[pltpu.VMEM((MOE_ROWS, d), F32), pltpu.SemaphoreType.DMA(()),
                            pltpu.SemaphoreType.DMA(())]),
        out_shape=jax.ShapeDtypeStruct((n_rows, d), F32),
        compiler_params=_cparams(("arbitrary",)),
        name="moe_dispatch",
    )(dest_flat, pend0, h1)


def _ffn_kernel(be_ref, nused_ref, xs_ref, wup_ref, bup_ref, wdn_ref, bdn_ref, y_ref, wup_sc, wdn_sc):
    b = pl.program_id(0)
    used = b < nused_ref[0]

    @pl.when(jnp.logical_and(used, jnp.logical_or(b == 0, be_ref[b] != be_ref[jnp.maximum(b - 1, 0)])))
    def _():
        wup_sc[...] = wup_ref[...].astype(BF16)
        wdn_sc[...] = wdn_ref[...].astype(BF16)

    @pl.when(used)
    def _():
        x = xs_ref[...].astype(BF16)
        hdn = jnp.dot(x, wup_sc[...], preferred_element_type=F32) + bup_ref[...]
        h_gate = jnp.minimum(hdn[:, :D_FF], SWIGLU_LIMIT)
        h_lin = jnp.clip(hdn[:, D_FF:], -SWIGLU_LIMIT, SWIGLU_LIMIT)
        act = (h_lin + 1.0) * (h_gate * jax.nn.sigmoid(SWIGLU_ALPHA * h_gate))
        y_ref[...] = jnp.dot(act.astype(BF16), wdn_sc[...], preferred_element_type=F32) + bdn_ref[...]

    @pl.when(b >= nused_ref[0])
    def _():
        y_ref[...] = jnp.zeros_like(y_ref)


def _ffn(blk_expert, n_used, xs, w_up, b_up, w_down, b_down):
    n_rows, d = xs.shape
    r = MOE_ROWS
    e = N_EXPERTS
    return pl.pallas_call(
        _ffn_kernel,
        grid_spec=pltpu.PrefetchScalarGridSpec(
            num_scalar_prefetch=2,
            grid=(n_rows // r,),
            in_specs=[pl.BlockSpec((r, d), lambda b, be, nu: (jnp.minimum(b, nu[0] - 1), 0)),
                      pl.BlockSpec((None, d, 2 * D_FF), lambda b, be, nu: (be[b], 0, 0)),
                      pl.BlockSpec((None, 1, 2 * D_FF), lambda b, be, nu: (be[b], 0, 0)),
                      pl.BlockSpec((None, D_FF, d), lambda b, be, nu: (be[b], 0, 0)),
                      pl.BlockSpec((None, 1, d), lambda b, be, nu: (be[b], 0, 0))],
            out_specs=pl.BlockSpec((r, d), lambda b, be, nu: (b, 0)),
            scratch_shapes=[pltpu.VMEM((d, 2 * D_FF), BF16), pltpu.VMEM((D_FF, d), BF16)]),
        out_shape=jax.ShapeDtypeStruct((n_rows, d), F32),
        compiler_params=_cparams(("arbitrary",)),
        name="moe_ffn",
    )(blk_expert, n_used, xs, w_up, b_up.reshape(e, 1, 2 * D_FF), w_down, b_down.reshape(e, 1, d))


def _combine_kernel(dest_ref, h1_ref, gate_ref, g2_ref, b2_ref, y_ref, o_ref, buf, sem):
    tm = h1_ref.shape[0]
    step = pl.program_id(0)
    slot = step % 2

    def gather(tile):
        sl = tile % 2

        def start(r, c):
            for kk in range(TOP_K):
                _row_copy(y_ref, dest_ref[(tile * tm + r) * TOP_K + kk], buf.at[sl, kk], r, sem.at[sl]).start()
            return c

        lax.fori_loop(0, tm, start, 0)

    @pl.when(step == 0)
    def _():
        gather(0)

    @pl.when(step + 1 < pl.num_programs(0))
    def _():
        gather(step + 1)

    def wait(r, c):
        for kk in range(TOP_K):
            _row_copy(y_ref, 0, buf.at[slot, kk], 0, sem.at[slot]).wait()
        return c

    lax.fori_loop(0, tm, wait, 0)
    gates = gate_ref[...]
    ffn = jnp.zeros(h1_ref.shape, F32)
    for kk in range(TOP_K):
        ffn = ffn + buf[slot, kk] * gates[:, kk:kk + 1]
    o_ref[...] = _layer_norm(DEEPNORM_ALPHA * h1_ref[...] + ffn, g2_ref[...], b2_ref[...])


def _combine(dest_flat, h1, gates, ln2_g, ln2_b, y):
    t, d = h1.shape
    tm = 256
    return pl.pallas_call(
        _combine_kernel,
        grid_spec=pltpu.PrefetchScalarGridSpec(
            num_scalar_prefetch=1,
            grid=(t // tm,),
            in_specs=[pl.BlockSpec((tm, d), lambda i, dest: (i, 0)),
                      pl.BlockSpec((tm, LANES), lambda i, dest: (i, 0)),
                      pl.BlockSpec((1, d), lambda i, dest: (0, 0)),
                      pl.BlockSpec((1, d), lambda i, dest: (0, 0)),
                      pl.BlockSpec(memory_space=pl.ANY)],
            out_specs=pl.BlockSpec((tm, d), lambda i, dest: (i, 0)),
            scratch_shapes=[pltpu.VMEM((2, TOP_K, tm, d), F32), pltpu.SemaphoreType.DMA((2,))]),
        out_shape=jax.ShapeDtypeStruct((t, d), F32),
        compiler_params=_cparams(("arbitrary",)),
        name="moe_combine",
    )(dest_flat, h1, gates, ln2_g.reshape(1, d), ln2_b.reshape(1, d), y)


def kernel(x, ln_in_g, ln_in_b, w_in, b_gate, lam_re, lam_im, log_dt, b_re, b_im, c_re, c_im, d_skip,
           w_glu, w_attn_proj, w_o, ln1_g, ln1_b, w_router, b_router, w_up, b_up, w_down, b_down,
           ln2_g, ln2_b):
    n_batch, seq, d = x.shape
    assert d == D_MODEL and seq % ATT_TQ == 0 and w_in.shape[0] == DEPTH
    t = n_batch * seq
    xt = x.reshape(t, d)

    u, qt, k, vt, qit, ki, wt, gp = _inproj(xt, ln_in_g, ln_in_b, w_in[0], n_batch, seq)
    ys = _ssm(u, n_batch, seq, lam_re[0], lam_im[0], log_dt[0], b_re[0], b_im[0], c_re[0], c_im[0])
    att = _dsa(qt, k, vt, qit, ki, wt, n_batch, seq)
    h1, idx, gates = _merge(xt, ln_in_g, ln_in_b, ys, u, d_skip[0], att, gp, b_gate[0], w_glu[0],
                            w_attn_proj[0], w_o[0], ln1_g[0], ln1_b[0], w_router[0], b_router[0])

    rank, counts = _rank(idx)
    counts = counts[0, :N_EXPERTS].astype(I32)
    padded = (counts + MOE_ROWS - 1) // MOE_ROWS * MOE_ROWS
    pends = jnp.cumsum(padded)
    pstarts = pends - padded
    top_idx = idx[:, :TOP_K]
    dest = (pstarts[top_idx] + rank[:, :TOP_K]).reshape(-1)
    n_rows = t * TOP_K + N_EXPERTS * MOE_ROWS
    blk_start = jnp.arange(n_rows // MOE_ROWS, dtype=I32) * MOE_ROWS
    blk_expert = jnp.minimum(jnp.sum((pends[None, :] <= blk_start[:, None]).astype(I32), axis=1),
                             N_EXPERTS - 1)
    n_used = (pends[-1:] // MOE_ROWS).astype(I32)

    xs = _dispatch(dest, jnp.concatenate([jnp.zeros((1,), I32), pends]), h1, n_rows)
    y = _ffn(blk_expert, n_used, xs, w_up[0], b_up[0], w_down[0], b_down[0])
    out = _combine(dest, h1, gates, ln2_g[0], ln2_b[0], y)
    return out.reshape(n_batch, seq, d)
```

```python
import functools
import math

import jax
import jax.numpy as jnp
import numpy as np
from jax import lax
from jax.experimental import pallas as pl
from jax.experimental.pallas import tpu as pltpu

F32 = jnp.float32
BF16 = jnp.bfloat16
I32 = jnp.int32

D_MODEL = 1024
SSM_WIDTH = 512
SSM_P = 16
SSM_G = 32
SSM_N = 64
SSM_CHUNK = 16
SSM_TG = 128 // SSM_P
N_HEADS = 8
HEAD_DIM = 64
ATTN_WIDTH = 512
IDX_HEADS = 8
IDX_DIM = 64
TOPK_KEYS = 256
N_EXPERTS = 32
TOP_K = 4
D_FF = 1024
SWIGLU_LIMIT = 7.0
SWIGLU_ALPHA = 1.702
LN_EPS = 1e-5
DEPTH = 1
DEEPNORM_ALPHA = (2 * DEPTH) ** 0.25

LANES = 128
SUBLANES = 8
ATT_TQ = 256
ATT_TK = 256
ATT_VROWS = HEAD_DIM + 16
ALIBI_SPLIT = 4
MOE_ROWS = 256
BISECT_STEPS = 32
NEG_BIG = -1e30
Q_SCALE = float(np.float32(HEAD_DIM ** -0.5 * math.log2(math.e)))
VMEM_LIMIT = 56 * 1024 * 1024


def _cparams(sem, vmem=VMEM_LIMIT):
    return pltpu.CompilerParams(dimension_semantics=sem, vmem_limit_bytes=vmem)


def _layer_norm(x, g, b):
    mu = jnp.mean(x, axis=-1, keepdims=True)
    xc = x - mu
    var = jnp.mean(xc * xc, axis=-1, keepdims=True)
    return xc * lax.rsqrt(var + LN_EPS) * g + b


def _inproj_kernel(x_ref, g_ref, b_ref, wu_ref, wqt_ref, wk_ref, wvt_ref, wqit_ref, wsm_ref, wsmt_ref, wg_ref,
                   ones_ref, u_ref, qt_ref, k_ref, vt_ref, qit_ref, ki_ref, wt_ref, gp_ref, *, seq):
    h = _layer_norm(x_ref[...], g_ref[...], b_ref[...])
    hb = h.astype(BF16)
    tm = x_ref.shape[0]

    def proj(w_ref):
        return jnp.dot(hb, w_ref[...], preferred_element_type=F32)

    def proj_t(w_ref):
        return lax.dot_general(w_ref[...], hb, (((1,), (1,)), ((), ())), preferred_element_type=F32)

    u = proj(wu_ref)
    for lt in range(SSM_WIDTH // LANES):
        u_ref[lt] = u[:, lt * LANES:(lt + 1) * LANES]
    qt_ref[...] = (proj_t(wqt_ref) * Q_SCALE).astype(BF16)
    shape = (tm, N_HEADS * LANES)
    pos = (pl.program_id(0) * tm + lax.broadcasted_iota(I32, shape, 0)) % seq
    sub = lax.broadcasted_iota(I32, shape, 1) % LANES - HEAD_DIM
    feat = jnp.where(jnp.logical_and(sub >= 0, sub < ALIBI_SPLIT), pos % ATT_TK,
                     jnp.where(jnp.logical_and(sub >= ALIBI_SPLIT, sub < 2 * ALIBI_SPLIT), pos // ATT_TK, 0))
    k_ref[...] = (proj(wk_ref) + feat.astype(F32)).astype(BF16)
    vt = (proj_t(wvt_ref) + ones_ref[...]).astype(BF16)
    for kb in range(tm // ATT_TK):
        vt_ref[kb] = vt[:, kb * ATT_TK:(kb + 1) * ATT_TK]
    qit_ref[...] = proj_t(wqit_ref).astype(BF16)
    ki_ref[...] = proj(wsm_ref)[:, :IDX_DIM].astype(BF16)
    wt_ref[...] = proj_t(wsmt_ref)[IDX_DIM:IDX_DIM + IDX_HEADS, :] * (IDX_DIM ** -0.5 * IDX_HEADS ** -0.5)
    gp_ref[...] = proj(wg_ref).astype(BF16)


def _inproj(xt, ln_g, ln_b, w_in, n_batch, seq):
    t, d = xt.shape
    tm = 512
    n_l = seq // tm
    o = 0
    ws = []
    for n in (SSM_WIDTH, ATTN_WIDTH, ATTN_WIDTH, ATTN_WIDTH, IDX_HEADS * IDX_DIM):
        ws.append(w_in[:, o:o + n].astype(BF16))
        o += n
    w_u, w_q, w_k, w_v, w_qi = ws
    w_k = jnp.pad(w_k.reshape(d, N_HEADS, HEAD_DIM),
                  ((0, 0), (0, 0), (0, LANES - HEAD_DIM))).reshape(d, N_HEADS * LANES)
    w_vt = jnp.pad(w_v.T.reshape(N_HEADS, HEAD_DIM, d),
                   ((0, 0), (0, ATT_VROWS - HEAD_DIM), (0, 0))).reshape(N_HEADS * ATT_VROWS, d)
    ones_col = jnp.zeros((N_HEADS, ATT_VROWS, 1), F32).at[:, HEAD_DIM, 0].set(1.0).reshape(-1, 1)
    n_small = IDX_DIM + IDX_HEADS
    w_small = jnp.pad(w_in[:, o:o + n_small], ((0, 0), (0, LANES - n_small))).astype(BF16)
    o += n_small
    w_gate = w_in[:, o:].astype(BF16)
    full = lambda a: pl.BlockSpec(a.shape, lambda i: (0,) * a.ndim)
    row = lambda n: pl.BlockSpec((tm, n), lambda i: (i, 0))
    col = lambda r: pl.BlockSpec((None, r, tm), lambda i: (i // n_l, 0, i % n_l))
    ins = [xt, ln_g.reshape(1, d), ln_b.reshape(1, d), w_u, w_q.T, w_k, w_vt, w_qi.T, w_small, w_small.T,
           w_gate, ones_col]
    nkb = seq // ATT_TK
    out_specs = [pl.BlockSpec((SSM_WIDTH // LANES, tm, LANES), lambda i: (0, i, 0)),
                 col(ATTN_WIDTH), row(N_HEADS * LANES),
                 pl.BlockSpec((None, tm // ATT_TK, N_HEADS * ATT_VROWS, ATT_TK),
                              lambda i: (i // n_l, i % n_l, 0, 0)),
                 col(IDX_HEADS * IDX_DIM), row(IDX_DIM), col(IDX_HEADS), row(2 * d)]
    out_shape = [jax.ShapeDtypeStruct((SSM_WIDTH // LANES, t, LANES), F32),
                 jax.ShapeDtypeStruct((n_batch, ATTN_WIDTH, seq), BF16),
                 jax.ShapeDtypeStruct((t, N_HEADS * LANES), BF16),
                 jax.ShapeDtypeStruct((n_batch, nkb, N_HEADS * ATT_VROWS, ATT_TK), BF16),
                 jax.ShapeDtypeStruct((n_batch, IDX_HEADS * IDX_DIM, seq), BF16),
                 jax.ShapeDtypeStruct((t, IDX_DIM), BF16),
                 jax.ShapeDtypeStruct((n_batch, IDX_HEADS, seq), F32),
                 jax.ShapeDtypeStruct((t, 2 * d), BF16)]
    return pl.pallas_call(
        functools.partial(_inproj_kernel, seq=seq),
        grid=(t // tm,),
        in_specs=[row(d)] + [full(a) for a in ins[1:]],
        out_specs=out_specs,
        out_shape=out_shape,
        compiler_params=_cparams(("parallel",)),
        name="inproj",
    )(*ins)


def _ssm_kernel(u_ref, lrr_ref, lir_ref, lrc_ref, lic_ref, ldt_ref, btr_ref, bti_ref, ctr_ref, cti_ref,
                tile_ref, spread_ref, y_ref, tg_sc, sin_sc, cout_sc, buf_a, buf_b, *, n_chunk, pad):
    cc, p, n, ng = SSM_CHUNK, SSM_P, SSM_N, SSM_TG
    hi = lax.Precision.HIGHEST
    width = cc * LANES

    def discretise(gl):
        dt = jnp.exp(ldt_ref[gl])
        return lrr_ref[gl] * dt, lir_ref[gl] * dt

    @pl.when(pl.program_id(1) == 0)
    def _():
        sin_sc[...] = jnp.zeros_like(sin_sc)
        w_rows = []
        for gl in range(ng):
            ar, ai = discretise(gl)
            mag = jnp.exp(ar)
            a_re, a_im = mag * jnp.cos(ai), mag * jnp.sin(ai)
            lr, li = lrr_ref[gl], lir_ref[gl]
            den = lr * lr + li * li
            f_re = ((a_re - 1.0) * lr + a_im * li) / den
            f_im = (a_im * lr - (a_re - 1.0) * li) / den
            bb_re = f_re * btr_ref[gl] - f_im * bti_ref[gl]
            bb_im = f_re * bti_ref[gl] + f_im * btr_ref[gl]
            dtc = jnp.exp(ldt_ref[gl])
            arc, aic = lrc_ref[gl] * dtc, lic_ref[gl] * dtc
            tau = (lax.broadcasted_iota(I32, (1, cc * p), 1) // p).astype(F32)
            rmag = jnp.exp(arc * tau)
            pw_re, pw_im = rmag * jnp.cos(aic * tau), rmag * jnp.sin(aic * tau)
            ct_re = jnp.dot(ctr_ref[gl], tile_ref[...], precision=hi, preferred_element_type=F32)
            ct_im = jnp.dot(cti_ref[gl], tile_ref[...], precision=hi, preferred_element_type=F32)
            r_re = pw_re * ct_re - pw_im * ct_im
            r_im = pw_re * ct_im + pw_im * ct_re
            m0 = (jnp.dot(bb_re, r_re, precision=hi, preferred_element_type=F32)
                  - jnp.dot(bb_im, r_im, precision=hi, preferred_element_type=F32))
            spread = jnp.dot(m0.astype(BF16), spread_ref[...], preferred_element_type=F32)
            w_rows.append(spread if gl == 0 else pltpu.roll(spread, gl * p, axis=1))
            e_pow = (cc - 1 - lax.broadcasted_iota(I32, (cc, 1), 0)).astype(F32)
            pmag = jnp.exp(ar * e_pow)
            ap_re, ap_im = pmag * jnp.cos(ai * e_pow), pmag * jnp.sin(ai * e_pow)
            for s in range(cc):
                pr, pi = ap_re[s:s + 1, :], ap_im[s:s + 1, :]
                sin_sc[s * LANES + gl * p:s * LANES + (gl + 1) * p, gl * LANES:(gl + 1) * LANES] = (
                    jnp.concatenate([bb_re * pr - bb_im * pi, bb_re * pi + bb_im * pr], axis=1).astype(BF16))
            mc = jnp.exp(arc)
            ac_re, ac_im = mc * jnp.cos(aic), mc * jnp.sin(aic)
            r1_re = ac_re * r_re - ac_im * r_im
            r1_im = ac_re * r_im + ac_im * r_re
            cout = jnp.concatenate([r1_re, -r1_im], axis=0).astype(BF16)
            cspread = jnp.dot(cout, spread_ref[...], preferred_element_type=F32)
            cout_sc[gl * LANES:(gl + 1) * LANES, :] = (
                cspread if gl == 0 else pltpu.roll(cspread, gl * p, axis=1)).astype(BF16)
        wb = jnp.concatenate(w_rows, axis=0).astype(BF16)
        for s in range(cc):
            tg_sc[s * LANES:(s + 1) * LANES, :] = wb if s == 0 else jnp.concatenate(
                [jnp.zeros((LANES, s * LANES), BF16), wb[:, :(cc - s) * LANES]], axis=1)

    @pl.when(jnp.logical_and(pl.program_id(0) == 0, pl.program_id(1) == 0))
    def _():
        buf_a[0:pad, :] = jnp.zeros((pad, ng * LANES), F32)
        buf_b[0:pad, :] = jnp.zeros((pad, ng * LANES), F32)

    u = u_ref[...].astype(BF16)
    y_intra = jnp.dot(u, tg_sc[...], preferred_element_type=F32)
    buf_a[pad:pad + n_chunk, :] = jnp.dot(u, sin_sc[...], preferred_element_type=F32)

    mults = []
    for gl in range(ng):
        ar, ai = discretise(gl)
        cm = jnp.exp(ar * cc)
        mults.append((cm * jnp.cos(ai * cc), cm * jnp.sin(ai * cc)))
    src, dst = buf_a, buf_b
    shift = 1
    while shift < n_chunk:
        for gl in range(ng):
            m_re, m_im = mults[gl]
            m_a = jnp.concatenate([m_re, m_re], axis=1)
            m_b = jnp.concatenate([-m_im, m_im], axis=1)
            ls = slice(gl * LANES, (gl + 1) * LANES)
            cur = src[pad:pad + n_chunk, ls]
            prev = src[pad - shift:pad - shift + n_chunk, ls]
            dst[pad:pad + n_chunk, ls] = cur + prev * m_a + pltpu.roll(prev, n, axis=1) * m_b
            mults[gl] = (m_re * m_re - m_im * m_im, 2.0 * m_re * m_im)
        src, dst = dst, src
        shift *= 2
    x_in = src[pad - 1:pad - 1 + n_chunk, :]
    y_ref[...] = y_intra + jnp.dot(x_in.astype(BF16), cout_sc[...], preferred_element_type=F32)


def _ssm(u_tiles, n_batch, seq, lam_re, lam_im, log_dt, b_re, b_im, c_re, c_im):
    g, p, n, cc, ng = SSM_G, SSM_P, SSM_N, SSM_CHUNK, SSM_TG
    n_tiles = g // ng
    n_chunk = seq // cc
    width = cc * LANES
    pad = max(n_chunk // 2, SUBLANES)
    u4 = u_tiles.reshape(n_tiles, n_batch, n_chunk, width)
    tau_p = np.arange(cc * p)
    tile_m = jnp.asarray((tau_p[None, :] % p == np.arange(p)[:, None]).astype(np.float32))
    spread_m = jnp.asarray((np.arange(width)[None, :] == ((tau_p // p) * LANES + tau_p % p)[:, None])
                           .astype(np.float32)).astype(BF16)
    params = [lam_re.reshape(g, 1, n), lam_im.reshape(g, 1, n), lam_re.reshape(g, n, 1),
              lam_im.reshape(g, n, 1), log_dt.reshape(g, 1, 1), b_re.transpose(0, 2, 1),
              b_im.transpose(0, 2, 1), c_re.transpose(0, 2, 1), c_im.transpose(0, 2, 1)]
    per_tile = lambda a: pl.BlockSpec((ng,) + a.shape[1:], lambda l, b: (l, 0, 0))
    const = lambda a: pl.BlockSpec(a.shape, lambda l, b: (0, 0))
    io_spec = pl.BlockSpec((None, None, n_chunk, width), lambda l, b: (l, b, 0, 0))
    y = pl.pallas_call(
        functools.partial(_ssm_kernel, n_chunk=n_chunk, pad=pad),
        grid=(n_tiles, n_batch),
        in_specs=[io_spec] + [per_tile(a) for a in params] + [const(tile_m), const(spread_m)],
        out_specs=io_spec,
        out_shape=jax.ShapeDtypeStruct(u4.shape, F32),
        scratch_shapes=[pltpu.VMEM((width, width), BF16), pltpu.VMEM((width, ng * LANES), BF16),
                        pltpu.VMEM((ng * LANES, width), BF16),
                        pltpu.VMEM((pad + n_chunk, ng * LANES), F32),
                        pltpu.VMEM((pad + n_chunk, ng * LANES), F32)],
        compiler_params=_cparams(("arbitrary", "arbitrary")),
        name="ssm",
    )(u4, *params, tile_m, spread_m)
    return y.reshape(n_tiles, n_batch * seq, LANES)


def _dsa_kernel(ki_ref, qit_ref, wt_ref, k_ref, qt_ref, vt_ref, slope_ref, o_ref,
                score_sc, qz_sc, acc_sc, bias_sc, lg_sc, p_sc, *, n_sel):
    tq, tk = ATT_TQ, ATT_TK
    i = pl.program_id(1)
    nj = i + 1
    row = lax.broadcasted_iota(I32, (tk, tq), 0)
    col = lax.broadcasted_iota(I32, (tk, tq), 1)
    causal_diag = row <= col

    def fold8(x, op):
        return op(x.reshape(tk // SUBLANES, SUBLANES, tq), axis=0)

    def score_block(j, carry):
        lo8, hi8 = carry
        kib = ki_ref[j]
        s = jnp.zeros((tk, tq), F32)
        for h in range(IDX_HEADS):
            d = jnp.dot(kib, qit_ref[h * IDX_DIM:(h + 1) * IDX_DIM, :], preferred_element_type=F32)
            s = s + jnp.maximum(d, 0.0) * wt_ref[h:h + 1, :]
        causal = jnp.logical_or(j < i, causal_diag)
        score_sc[j] = jnp.where(causal, s, -jnp.inf)
        return (jnp.minimum(lo8, fold8(jnp.where(causal, s, jnp.inf), jnp.min)),
                jnp.maximum(hi8, fold8(jnp.where(causal, s, -jnp.inf), jnp.max)))

    lo8, hi8 = lax.fori_loop(0, nj, score_block, (jnp.full((SUBLANES, tq), jnp.inf, F32),
                                                  jnp.full((SUBLANES, tq), -jnp.inf, F32)))
    s_min = jnp.min(lo8, axis=0, keepdims=True)
    s_max = jnp.max(hi8, axis=0, keepdims=True)

    def count_ge(cand):
        def body(p, acc):
            j1 = jnp.minimum(2 * p + 1, nj - 1)
            w1 = jnp.where(2 * p + 1 < nj, 1.0, 0.0)
            return (acc + fold8(jnp.where(score_sc[2 * p] >= cand, 1.0, 0.0), jnp.sum)
                    + fold8(jnp.where(score_sc[j1] >= cand, w1, 0.0), jnp.sum))
        acc = lax.fori_loop(0, (nj + 1) // 2, body, jnp.zeros((SUBLANES, tq), F32))
        return acc.sum(axis=0, keepdims=True)

    def bisect_cond(state):
        it, n_open = state[0], state[1]
        return jnp.logical_and(it < BISECT_STEPS, n_open > 0)

    def bisect_step(state):
        it, _, open_q, lo, hi = state
        for _ in range(2):
            mid = lo + (hi - lo) * 0.5
            cnt = count_ge(mid)
            ok = cnt >= n_sel
            open_q = jnp.where(cnt == n_sel, 0, open_q)
            lo, hi = jnp.where(ok, mid, lo), jnp.where(ok, hi, mid)
        return it + 2, jnp.sum(open_q), open_q, lo, hi

    hi0 = s_max + jnp.abs(s_max) * 2.0 ** -20 + 1e-30
    state = (jnp.int32(0), jnp.int32(1), jnp.ones((1, tq), I32), s_min, hi0)
    thr_lo, thr_hi = lax.while_loop(bisect_cond, bisect_step, state)[3:]
    need_eq = n_sel - count_ge(thr_hi)

    for h in range(N_HEADS):
        qz_sc[h] = jnp.concatenate([qt_ref[h * HEAD_DIM:(h + 1) * HEAD_DIM, :], slope_ref[h]], axis=0)
    acc_sc[...] = jnp.zeros_like(acc_sc)
    stril = (row > col).astype(BF16)
    n_chunk = 4
    rows_c = tk // n_chunk

    def logits_block(j, live, eq_seen, m_old, half):
        sb = score_sc[j]
        above = sb >= thr_hi
        eq = jnp.logical_and(sb >= thr_lo, jnp.logical_not(above))
        eq_f = jnp.where(eq, live, 0.0).astype(BF16)
        before = jnp.dot(stril, eq_f, preferred_element_type=F32) + eq_seen
        sel = jnp.logical_or(above, jnp.logical_and(eq, before < need_eq))
        bias_sc[half] = jnp.where(jnp.logical_and(sel, live > 0.0), 0.0, -jnp.inf)
        block_max = []
        for h in range(N_HEADS):
            lg = jnp.dot(k_ref[j, :, h * LANES:(h + 1) * LANES], qz_sc[h],
                         preferred_element_type=F32) + bias_sc[half]
            lg_sc[half * N_HEADS + h] = lg
            block_max.append(jnp.max(lg, axis=0, keepdims=True))
        m_new = jnp.maximum(m_old, jnp.concatenate(block_max, axis=0))
        return eq_seen + jnp.sum(eq_f.astype(F32), axis=0, keepdims=True), m_new, jnp.exp2(m_old - m_new)

    def weigh_block(j, m_new, alpha, half):
        for h in range(N_HEADS):
            for c in range(n_chunk):
                rs = slice(c * rows_c, (c + 1) * rows_c)
                p_sc[half * N_HEADS + h, rs, :] = jnp.exp2(
                    lg_sc[half * N_HEADS + h, rs, :] - m_new[h:h + 1, :]).astype(BF16)
            pv = jnp.dot(vt_ref[j, h * ATT_VROWS:(h + 1) * ATT_VROWS, :], p_sc[half * N_HEADS + h],
                         preferred_element_type=F32)
            sl = slice(h * ATT_VROWS, (h + 1) * ATT_VROWS)
            acc_sc[sl, :] = alpha[h:h + 1, :] * acc_sc[sl, :] + pv

    def attend_pair(p, carry):
        eq_seen, m_run = carry
        j0 = 2 * p
        j1 = jnp.minimum(j0 + 1, nj - 1)
        live1 = jnp.where(j0 + 1 < nj, 1.0, 0.0)
        eq_seen, m0, a0 = logits_block(j0, jnp.float32(1.0), eq_seen, m_run, 0)
        eq_seen, m1, a1 = logits_block(j1, live1, eq_seen, m0, 1)
        weigh_block(j0, m0, a0, 0)
        weigh_block(j1, m1, a1, 1)
        return eq_seen, m1

    lax.fori_loop(0, (nj + 1) // 2, attend_pair,
                  (jnp.zeros((1, tq), F32), jnp.full((N_HEADS, tq), NEG_BIG, F32)))
    for h in range(N_HEADS):
        base = h * ATT_VROWS
        o_ref[h * HEAD_DIM:(h + 1) * HEAD_DIM, :] = (
            acc_sc[base:base + HEAD_DIM, :] / acc_sc[base + HEAD_DIM:base + HEAD_DIM + 1, :]).astype(o_ref.dtype)


def _alibi_rows(tq):
    rows = []
    for h in range(N_HEADS):
        rest = 2.0 ** (-8.0 * (h + 1) / N_HEADS) * Q_SCALE * HEAD_DIM ** 0.5
        pieces = []
        for _ in range(ALIBI_SPLIT):
            piece = float(np.asarray(rest, dtype=BF16).astype(np.float32))
            pieces.append(piece)
            rest -= piece
        col = pieces + [ATT_TK * x for x in pieces] + [0.0] * (HEAD_DIM - 2 * ALIBI_SPLIT)
        rows.append(col)
    return jnp.broadcast_to(jnp.asarray(np.asarray(rows, np.float32))[:, :, None],
                            (N_HEADS, HEAD_DIM, tq)).astype(BF16)


def _dsa(qt, k, vt, qit, ki, wt, n_batch, seq):
    tq, tk = ATT_TQ, ATT_TK
    nkb = seq // tk
    n_sel = min(TOPK_KEYS, seq // 4)
    kb = k.reshape(n_batch, nkb, tk, N_HEADS * LANES)
    kib = ki.reshape(n_batch, nkb, tk, IDX_DIM)
    per_batch = lambda shape: pl.BlockSpec((None,) + shape, lambda b, i: (b, 0, 0, 0),
                                           pipeline_mode=pl.Buffered(1))
    per_q = lambda r: pl.BlockSpec((None, r, tq), lambda b, i: (b, 0, i))
    return pl.pallas_call(
        functools.partial(_dsa_kernel, n_sel=n_sel),
        grid=(n_batch, seq // tq),
        in_specs=[per_batch((nkb, tk, IDX_DIM)), per_q(IDX_HEADS * IDX_DIM), per_q(IDX_HEADS),
                  per_batch((nkb, tk, N_HEADS * LANES)), per_q(ATTN_WIDTH),
                  per_batch((nkb, N_HEADS * ATT_VROWS, tk)),
                  pl.BlockSpec((N_HEADS, HEAD_DIM, tq), lambda b, i: (0, 0, 0))],
        out_specs=per_q(ATTN_WIDTH),
        out_shape=jax.ShapeDtypeStruct((n_batch, ATTN_WIDTH, seq), BF16),
        scratch_shapes=[pltpu.VMEM((nkb, tk, tq), F32), pltpu.VMEM((N_HEADS, LANES, tq), BF16),
                        pltpu.VMEM((N_HEADS * ATT_VROWS, tq), F32),
                        pltpu.VMEM((2, tk, tq), F32), pltpu.VMEM((2 * N_HEADS, tk, tq), F32),
                        pltpu.VMEM((2 * N_HEADS, tk, tq), BF16)],
        compiler_params=_cparams(("parallel", "arbitrary")),
        name="dsa",
    )(kib, qit, wt, kb, qt, vt, _alibi_rows(tq))


def _merge_kernel(x_ref, g0_ref, b0_ref, ys_ref, u_ref, dsk_ref, att_ref, gp_ref, bg_ref,
                  wglu_ref, wap_ref, wo_ref, g1_ref, b1_ref, wrh_ref, wrl_ref, br_ref,
                  h1_ref, idx_ref, gate_ref):
    d = D_MODEL
    h = _layer_norm(x_ref[...], g0_ref[...], b0_ref[...])
    tiles = range(SSM_WIDTH // LANES)
    ys = jnp.concatenate([ys_ref[lt] for lt in tiles], axis=1)
    u = jnp.concatenate([u_ref[lt] for lt in tiles], axis=1)
    y = jax.nn.gelu(ys + dsk_ref[...] * u)
    glu = jnp.dot(y.astype(BF16), wglu_ref[...], preferred_element_type=F32)
    y_ssm = glu[:, :d] * jax.nn.sigmoid(glu[:, d:])
    y_att = lax.dot_general(att_ref[...], wap_ref[...], (((0,), (0,)), ((), ())),
                            preferred_element_type=F32)
    gs = jax.nn.sigmoid(gp_ref[...].astype(F32) + bg_ref[...])
    mixed = gs[:, :d] * y_ssm + gs[:, d:] * y_att
    mix = jnp.dot(mixed.astype(BF16), wo_ref[...], preferred_element_type=F32)
    h1 = _layer_norm(DEEPNORM_ALPHA * h + mix, g1_ref[...], b1_ref[...])
    h1_ref[...] = h1
    h1_hi = h1.astype(BF16)
    h1_lo = (h1 - h1_hi.astype(F32)).astype(BF16)
    logits = (jnp.dot(h1_hi, wrh_ref[...], preferred_element_type=F32)
              + jnp.dot(h1_hi, wrl_ref[...], preferred_element_type=F32)
              + jnp.dot(h1_lo, wrh_ref[...], preferred_element_type=F32)) + br_ref[...]
    lane = lax.broadcasted_iota(I32, logits.shape, 1)
    logits = jnp.where(lane < N_EXPERTS, logits, -jnp.inf)
    idx_out = jnp.zeros(logits.shape, I32)
    val_out = jnp.zeros(logits.shape, F32)
    vals = []
    for kk in range(TOP_K):
        vmax = jnp.max(logits, axis=-1, keepdims=True)
        imax = jnp.min(jnp.where(logits == vmax, lane, LANES), axis=-1, keepdims=True)
        idx_out = jnp.where(lane == kk, imax, idx_out)
        vals.append(vmax)
        logits = jnp.where(lane == imax, -jnp.inf, logits)
    es = [jnp.exp(vk - vals[0]) for vk in vals]
    den = es[0] + es[1] + es[2] + es[3]
    for kk in range(TOP_K):
        val_out = jnp.where(lane == kk, es[kk] / den, val_out)
    idx_ref[...] = idx_out
    gate_ref[...] = val_out


def _merge(xt, ln0_g, ln0_b, ys, u, d_skip, att, gp, b_gate, w_glu, w_ap, w_o, ln1_g, ln1_b,
           w_router, b_router):
    t, d = xt.shape
    tm = 256
    wr = jnp.pad(w_router, ((0, 0), (0, LANES - N_EXPERTS)))
    wr_hi = wr.astype(BF16)
    wr_lo = (wr - wr_hi.astype(F32)).astype(BF16)
    br = jnp.pad(b_router, (0, LANES - N_EXPERTS)).reshape(1, LANES)
    ins = [xt, ln0_g.reshape(1, d), ln0_b.reshape(1, d), ys, u, d_skip.reshape(1, SSM_WIDTH), att, gp,
           b_gate.reshape(1, 2 * d), w_glu.astype(BF16), w_ap.astype(BF16), w_o.astype(BF16),
           ln1_g.reshape(1, d), ln1_b.reshape(1, d), wr_hi, wr_lo, br]
    tiled = {0, 7}
    n_l = att.shape[2] // tm
    in_specs = []
    for n, a in enumerate(ins):
        if n in tiled:
            in_specs.append(pl.BlockSpec((tm, a.shape[1]), lambda i: (i, 0)))
        elif n in (3, 4):
            in_specs.append(pl.BlockSpec((a.shape[0], tm, LANES), lambda i: (0, i, 0)))
        elif n == 6:
            in_specs.append(pl.BlockSpec((None, ATTN_WIDTH, tm), lambda i: (i // n_l, 0, i % n_l)))
        else:
            in_specs.append(pl.BlockSpec(a.shape, lambda i: (0, 0)))
    return pl.pallas_call(
        _merge_kernel,
        grid=(t // tm,),
        in_specs=in_specs,
        out_specs=[pl.BlockSpec((tm, d), lambda i: (i, 0)), pl.BlockSpec((tm, LANES), lambda i: (i, 0)),
                   pl.BlockSpec((tm, LANES), lambda i: (i, 0))],
        out_shape=[jax.ShapeDtypeStruct((t, d), F32), jax.ShapeDtypeStruct((t, LANES), I32),
                   jax.ShapeDtypeStruct((t, LANES), F32)],
        compiler_params=_cparams(("parallel",)),
        name="merge",
    )(*ins)


def _rank_kernel(idx_ref, rank_ref, cnt_ref, run_sc):
    tm = idx_ref.shape[0]

    @pl.when(pl.program_id(0) == 0)
    def _():
        run_sc[...] = jnp.zeros_like(run_sc)

    idx = idx_ref[...]
    lane = lax.broadcasted_iota(I32, (tm, LANES), 1)
    hots = [lane == idx[:, kk:kk + 1] for kk in range(TOP_K)]
    multi = sum(hh.astype(F32) for hh in hots)
    r_i = lax.broadcasted_iota(I32, (tm, tm), 0)
    c_i = lax.broadcasted_iota(I32, (tm, tm), 1)
    before = jnp.dot((c_i < r_i).astype(BF16), multi.astype(BF16), preferred_element_type=F32)
    base = before + run_sc[...]
    out = jnp.zeros((tm, LANES), I32)
    for kk in range(TOP_K):
        rk = jnp.sum(jnp.where(hots[kk], base, 0.0), axis=-1, keepdims=True)
        out = jnp.where(lane == kk, rk.astype(I32), out)
    rank_ref[...] = out
    run_sc[...] = run_sc[...] + jnp.sum(multi, axis=0, keepdims=True)
    cnt_ref[...] = run_sc[...]


def _rank(idx):
    t = idx.shape[0]
    tm = 512
    return pl.pallas_call(
        _rank_kernel,
        grid=(t // tm,),
        in_specs=[pl.BlockSpec((tm, LANES), lambda i: (i, 0))],
        out_specs=[pl.BlockSpec((tm, LANES), lambda i: (i, 0)), pl.BlockSpec((1, LANES), lambda i: (0, 0))],
        out_shape=[jax.ShapeDtypeStruct((t, LANES), I32), jax.ShapeDtypeStruct((1, LANES), F32)],
        scratch_shapes=[pltpu.VMEM((1, LANES), F32)],
        compiler_params=_cparams(("arbitrary",)),
        name="moe_rank",
    )(idx)


def _row_copy(src_ref, src_row, dst_ref, dst_row, sem):
    return pltpu.make_async_copy(src_ref.at[pl.ds(src_row, 1), :], dst_ref.at[pl.ds(dst_row, 1), :], sem)


def _dispatch_kernel(dest_ref, pend_ref, h_ref, xs_ref, zero_sc, sem, zsem):
    tm = h_ref.shape[0]
    base = pl.program_id(0) * tm * TOP_K

    @pl.when(pl.program_id(0) == 0)
    def _():
        zero_sc[...] = jnp.zeros_like(zero_sc)

        def zero_copy(e):
            start = pl.multiple_of(pend_ref[e + 1] - MOE_ROWS, MOE_ROWS)
            return pltpu.make_async_copy(zero_sc, xs_ref.at[pl.ds(start, MOE_ROWS), :], zsem)

        for e in range(N_EXPERTS):
            @pl.when(pend_ref[e + 1] > pend_ref[e])
            def _():
                zero_copy(e).start()
        for e in range(N_EXPERTS):
            @pl.when(pend_ref[e + 1] > pend_ref[e])
            def _():
                zero_copy(e).wait()

        def tail_copy(blk):
            return pltpu.make_async_copy(
                zero_sc, xs_ref.at[pl.ds(pl.multiple_of(blk * MOE_ROWS, MOE_ROWS), MOE_ROWS), :], zsem)

        first_free = pend_ref[N_EXPERTS] // MOE_ROWS
        n_blocks = xs_ref.shape[0] // MOE_ROWS
        lax.fori_loop(first_free, n_blocks, lambda blk, c: (tail_copy(blk).start(), c)[1], 0)
        lax.fori_loop(first_free, n_blocks, lambda blk, c: (tail_copy(blk).wait(), c)[1], 0)

    def start(r, c):
        for kk in range(TOP_K):
            _row_copy(h_ref, r, xs_ref, dest_ref[base + r * TOP_K + kk], sem).start()
        return c

    lax.fori_loop(0, tm, start, 0)

    def wait(r, c):
        for kk in range(TOP_K):
            _row_copy(h_ref, 0, xs_ref, 0, sem).wait()
        return c

    lax.fori_loop(0, tm, wait, 0)


def _dispatch(dest_flat, pend0, h1, n_rows):
    t, d = h1.shape
    tm = 512
    return pl.pallas_call(
        _dispatch_kernel,
        grid_spec=pltpu.PrefetchScalarGridSpec(
            num_scalar_prefetch=2,
            grid=(t // tm,),
            in_specs=[pl.BlockSpec((tm, d), lambda i, dest, pend: (i, 0))],
            out_specs=pl.BlockSpec(memory_space=pl.ANY),
            scratch_shapes=[pltpu.VMEM((MOE_ROWS, d), F32), pltpu.SemaphoreType.DMA(()),
                            pltpu.SemaphoreType.DMA(())]),
        out_shape=jax.ShapeDtypeStruct((n_rows, d), F32),
        compiler_params=_cparams(("arbitrary",)),
        name="moe_dispatch",
    )(dest_flat, pend0, h1)


def _ffn_kernel(be_ref, nused_ref, xs_ref, wup_ref, bup_ref, wdn_ref, bdn_ref, y_ref, wup_sc, wdn_sc):
    b = pl.program_id(0)
    used = b < nused_ref[0]

    @pl.when(jnp.logical_and(used, jnp.logical_or(b == 0, be_ref[b] != be_ref[jnp.maximum(b - 1, 0)])))
    def _():
        wup_sc[...] = wup_ref[...].astype(BF16)
        wdn_sc[...] = wdn_ref[...].astype(BF16)

    @pl.when(used)
    def _():
        x = xs_ref[...].astype(BF16)
        hdn = jnp.dot(x, wup_sc[...], preferred_element_type=F32) + bup_ref[...]
        h_gate = jnp.minimum(hdn[:, :D_FF], SWIGLU_LIMIT)
        h_lin = jnp.clip(hdn[:, D_FF:], -SWIGLU_LIMIT, SWIGLU_LIMIT)
        act = (h_lin + 1.0) * (h_gate * jax.nn.sigmoid(SWIGLU_ALPHA * h_gate))
        y_ref[...] = jnp.dot(act.astype(BF16), wdn_sc[...], preferred_element_type=F32) + bdn_ref[...]

    @pl.when(b >= nused_ref[0])
    def _():
        y_ref[...] = jnp.zeros_like(y_ref)


def _ffn(blk_expert, n_used, xs, w_up, b_up, w_down, b_down):
    n_rows, d = xs.shape
    r = MOE_ROWS
    e = N_EXPERTS
    return pl.pallas_call(
        _ffn_kernel,
        grid_spec=pltpu.PrefetchScalarGridSpec(
            num_scalar_prefetch=2,
            grid=(n_rows // r,),
            in_specs=[pl.BlockSpec((r, d), lambda b, be, nu: (jnp.minimum(b, nu[0] - 1), 0)),
                      pl.BlockSpec((None, d, 2 * D_FF), lambda b, be, nu: (be[b], 0, 0)),
                      pl.BlockSpec((None, 1, 2 * D_FF), lambda b, be, nu: (be[b], 0, 0)),
                      pl.BlockSpec((None, D_FF, d), lambda b, be, nu: (be[b], 0, 0)),
                      pl.BlockSpec((None, 1, d), lambda b, be, nu: (be[b], 0, 0))],
            out_specs=pl.BlockSpec((r, d), lambda b, be, nu: (b, 0)),
            scratch_shapes=[pltpu.VMEM((d, 2 * D_FF), BF16), pltpu.VMEM((D_FF, d), BF16)]),
        out_shape=jax.ShapeDtypeStruct((n_rows, d), F32),
        compiler_params=_cparams(("arbitrary",)),
        name="moe_ffn",
    )(blk_expert, n_used, xs, w_up, b_up.reshape(e, 1, 2 * D_FF), w_down, b_down.reshape(e, 1, d))


def _combine_kernel(dest_ref, h1_ref, gate_ref, g2_ref, b2_ref, y_ref, o_ref, buf, sem):
    tm = h1_ref.shape[0]
    step = pl.program_id(0)
    slot = step % 2

    def gather(tile):
        sl = tile % 2

        def start(r, c):
            for kk in range(TOP_K):
                _row_copy(y_ref, dest_ref[(tile * tm + r) * TOP_K + kk], buf.at[sl, kk], r, sem.at[sl]).start()
            return c

        lax.fori_loop(0, tm, start, 0)

    @pl.when(step == 0)
    def _():
        gather(0)

    @pl.when(step + 1 < pl.num_programs(0))
    def _():
        gather(step + 1)

    def wait(r, c):
        for kk in range(TOP_K):
            _row_copy(y_ref, 0, buf.at[slot, kk], 0, sem.at[slot]).wait()
        return c

    lax.fori_loop(0, tm, wait, 0)
    gates = gate_ref[...]
    ffn = jnp.zeros(h1_ref.shape, F32)
    for kk in range(TOP_K):
        ffn = ffn + buf[slot, kk] * gates[:, kk:kk + 1]
    o_ref[...] = _layer_norm(DEEPNORM_ALPHA * h1_ref[...] + ffn, g2_ref[...], b2_ref[...])


def _combine(dest_flat, h1, gates, ln2_g, ln2_b, y):
    t, d = h1.shape
    tm = 256
    return pl.pallas_call(
        _combine_kernel,
        grid_spec=pltpu.PrefetchScalarGridSpec(
            num_scalar_prefetch=1,
            grid=(t // tm,),
            in_specs=[pl.BlockSpec((tm, d), lambda i, dest: (i, 0)),
                      pl.BlockSpec((tm, LANES), lambda i, dest: (i, 0)),
                      pl.BlockSpec((1, d), lambda i, dest: (0, 0)),
                      pl.BlockSpec((1, d), lambda i, dest: (0, 0)),
                      pl.BlockSpec(memory_space=pl.ANY)],
            out_specs=pl.BlockSpec((tm, d), lambda i, dest: (i, 0)),
            scratch_shapes=[pltpu.VMEM((2, TOP_K, tm, d), F32), pltpu.SemaphoreType.DMA((2,))]),
        out_shape=jax.ShapeDtypeStruct((t, d), F32),
        compiler_params=_cparams(("arbitrary",)),
        name="moe_combine",
    )(dest_flat, h1, gates, ln2_g.reshape(1, d), ln2_b.reshape(1, d), y)


def kernel(x, ln_in_g, ln_in_b, w_in, b_gate, lam_re, lam_im, log_dt, b_re, b_im, c_re, c_im, d_skip,
           w_glu, w_attn_proj, w_o, ln1_g, ln1_b, w_router, b_router, w_up, b_up, w_down, b_down,
           ln2_g, ln2_b):
    n_batch, seq, d = x.shape
    assert d == D_MODEL and seq % ATT_TQ == 0 and w_in.shape[0] == DEPTH
    t = n_batch * seq
    xt = x.reshape(t, d)

    u, qt, k, vt, qit, ki, wt, gp = _inproj(xt, ln_in_g, ln_in_b, w_in[0], n_batch, seq)
    ys = _ssm(u, n_batch, seq, lam_re[0], lam_im[0], log_dt[0], b_re[0], b_im[0], c_re[0], c_im[0])
    att = _dsa(qt, k, vt, qit, ki, wt, n_batch, seq)
    h1, idx, gates = _merge(xt, ln_in_g, ln_in_b, ys, u, d_skip[0], att, gp, b_gate[0], w_glu[0],
                            w_attn_proj[0], w_o[0], ln1_g[0], ln1_b[0], w_router[0], b_router[0])

    rank, counts = _rank(idx)
    counts = counts[0, :N_EXPERTS].astype(I32)
    padded = (counts + MOE_ROWS - 1) // MOE_ROWS * MOE_ROWS
    pends = jnp.cumsum(padded)
    pstarts = pends - padded
    top_idx = idx[:, :TOP_K]
    dest = (pstarts[top_idx] + rank[:, :TOP_K]).reshape(-1)
    n_rows = t * TOP_K + N_EXPERTS * MOE_ROWS
    blk_start = jnp.arange(n_rows // MOE_ROWS, dtype=I32) * MOE_ROWS
    blk_expert = jnp.minimum(jnp.sum((pends[None, :] <= blk_start[:, None]).astype(I32), axis=1),
                             N_EXPERTS - 1)
    n_used = (pends[-1:] // MOE_ROWS).astype(I32)

    xs = _dispatch(dest, jnp.concatenate([jnp.zeros((1,), I32), pends]), h1, n_rows)
    y = _ffn(blk_expert, n_used, xs, w_up[0], b_up[0], w_down[0], b_down[0])
    out = _combine(dest, h1, gates, ln2_g[0], ln2_b[0], y)
    return out.reshape(n_batch, seq, d)
```
